```python
import jax
import jax.numpy as jnp
from jax import lax
import numpy as np

D_MODEL = 1024
BATCH = 1
SEQ = 16384
DEPTH = 1
DEC_BATCH = 16
DEC_SEQ = 16
PAST_LEN = 4096

CHUNK = 64
Q_BLOCK = 128
HEAD_DIM = 64
N_HEADS_A = 8
N_HEADS_B = 8
D_A = N_HEADS_A * HEAD_DIM
D_B = N_HEADS_B * HEAD_DIM
N_IDX_HEADS = 8
IDX_DIM = 64
IDX_SCALE = (N_IDX_HEADS * IDX_DIM) ** -0.5
TOPK_MAX = 256
ROPE_THETA = 500000.0
ROT_DIM = HEAD_DIM // 4
N_GROUPS = 4
EXPERTS_PER_GROUP = 8
N_EXPERTS = N_GROUPS * EXPERTS_PER_GROUP
TOPK_IN_GROUP = 2
D_FF_EXPERT = D_MODEL // 4
RMS_EPS = 1e-6
IN_SPLIT = (D_A, D_A, D_A, N_IDX_HEADS * IDX_DIM, IDX_DIM, N_IDX_HEADS, D_B, D_B, D_B, D_MODEL, D_MODEL)
D_IN = sum(IN_SPLIT)

kernel_name = 'hybrid_dsa_stickbreaking_hmoe_step'


def rmsnorm(x, g):
    xf = x.astype(jnp.float32)
    r = lax.rsqrt(jnp.mean(xf * xf, axis=-1, keepdims=True) + RMS_EPS)
    return (xf * r).astype(x.dtype) * g


def rope_partial(x, pos):
    half = ROT_DIM // 2
    inv_freq = ROPE_THETA ** (-jnp.arange(half, dtype=jnp.float32) / half)
    ang = pos.astype(jnp.float32)[:, None] * inv_freq[None, :]
    cos = jnp.cos(ang)[None, :, None, :]
    sin = jnp.sin(ang)[None, :, None, :]
    xf = x.astype(jnp.float32)
    x1 = xf[..., :half]
    x2 = xf[..., half:ROT_DIM]
    out = jnp.concatenate([x1 * cos - x2 * sin, x2 * cos + x1 * sin, xf[..., ROT_DIM:]], axis=-1)
    return out.astype(x.dtype)


def split_in_proj(p):
    b, t, _ = p.shape
    offs = np.cumsum(IN_SPLIT)[:-1].tolist()
    q_a, k_a, v_a, q_i, k_i, w_i, q_b, k_b, v_b, g_a, g_b = jnp.split(p, offs, axis=-1)
    q_a = q_a.reshape(b, t, N_HEADS_A, HEAD_DIM)
    k_a = k_a.reshape(b, t, N_HEADS_A, HEAD_DIM)
    v_a = v_a.reshape(b, t, N_HEADS_A, HEAD_DIM)
    q_i = q_i.reshape(b, t, N_IDX_HEADS, IDX_DIM)
    q_b = q_b.reshape(b, t, N_HEADS_B, HEAD_DIM)
    k_b = k_b.reshape(b, t, N_HEADS_B, HEAD_DIM)
    v_b = v_b.reshape(b, t, N_HEADS_B, HEAD_DIM)
    return q_a, k_a, v_a, q_i, k_i, w_i, q_b, k_b, v_b, g_a, g_b


def sweep_query_blocks(fn, qs, pos):
    t = pos.shape[0]
    if t <= Q_BLOCK or t % Q_BLOCK != 0:
        return fn(qs, pos)
    nb = t // Q_BLOCK

    def to_blocks(a):
        return jnp.swapaxes(a.reshape((a.shape[0], nb, Q_BLOCK) + a.shape[2:]), 0, 1)

    out = lax.map(lambda xs: fn(xs[0], xs[1]), (tuple(to_blocks(a) for a in qs), pos.reshape(nb, Q_BLOCK)))
    out = jnp.swapaxes(out, 0, 1)
    return out.reshape((out.shape[0], t) + out.shape[3:])


def dsa_attention(q, q_idx, w_idx, q_pos, k, v, k_idx, topk):
    L = k.shape[1]
    key_pos = jnp.arange(L, dtype=jnp.int32)
    limit = (q_pos // CHUNK + 1) * CHUNK
    admissible = key_pos[None, :] < limit[:, None]
    rel = jax.nn.relu(jnp.einsum('bqhd,bld->bqhl', q_idx, k_idx).astype(jnp.float32))
    score = jnp.einsum('bqh,bqhl->bql', w_idx.astype(jnp.float32), rel) * IDX_SCALE
    score = jnp.where(admissible[None], score, -jnp.inf)
    _, sel = lax.top_k(score, topk)
    valid = sel < limit[None, :, None]
    gather = jax.vmap(lambda rows, idx: rows[idx])
    k_sel = gather(k, sel)
    v_sel = gather(v, sel)
    logits = jnp.einsum('bqhd,bqkhd->bhqk', q, k_sel).astype(jnp.float32) * (HEAD_DIM ** -0.5)
    logits = jnp.where(valid[:, None], logits, -jnp.inf)
    probs = jax.nn.softmax(logits, axis=-1)
    return jnp.einsum('bhqk,bqkhd->bqhd', probs.astype(v.dtype), v_sel)


def stick_breaking_attention(q, q_pos, k, v):
    L = k.shape[1]
    key_pos = jnp.arange(L, dtype=jnp.int32)
    strict = (key_pos[None, :] < q_pos[:, None])[None, None]
    z = jnp.einsum('bqhd,blhd->bhql', q, k).astype(jnp.float32) * (HEAD_DIM ** -0.5)
    log_stay = jnp.where(strict, jax.nn.log_sigmoid(-z), 0.0)
    log_rest = lax.cumsum(log_stay, axis=3, reverse=True) - log_stay
    weights = jnp.where(strict, jnp.exp(jax.nn.log_sigmoid(z) + log_rest), 0.0)
    return jnp.einsum('bhql,blhd->bqhd', weights.astype(v.dtype), v)


def hierarchical_moe(x, w_rg, b_rg, w_re, b_re, w_eg, w_eu, w_ed):
    g_logits = (x @ w_rg).astype(jnp.float32) + b_rg.astype(jnp.float32)
    g_sel = jnp.argmax(g_logits, axis=-1)
    g_w = jnp.take_along_axis(jax.nn.softmax(g_logits, axis=-1), g_sel[:, None], axis=-1)
    e_logits = ((x @ w_re).astype(jnp.float32) + b_re.astype(jnp.float32)).reshape(-1, N_GROUPS, EXPERTS_PER_GROUP)
    e_in = jnp.take_along_axis(e_logits, g_sel[:, None, None], axis=1)[:, 0]
    top_v, top_i = lax.top_k(e_in, TOPK_IN_GROUP)
    gate = jax.nn.softmax(top_v, axis=-1) * g_w
    expert_id = g_sel[:, None] * EXPERTS_PER_GROUP + top_i
    combine = jnp.einsum('nk,nke->ne', gate, jax.nn.one_hot(expert_id, N_EXPERTS, dtype=jnp.float32))
    hid = jax.nn.silu(jnp.einsum('nd,edf->nef', x, w_eg)) * jnp.einsum('nd,edf->nef', x, w_eu)
    hid = hid * combine[:, :, None].astype(hid.dtype)
    return jnp.einsum('nef,efd->nd', hid, w_ed)


def moe_tokens(hn, moe_w):
    n = hn.shape[0]
    if n > Q_BLOCK and n % Q_BLOCK == 0:
        out = lax.map(lambda blk: hierarchical_moe(blk, *moe_w), hn.reshape(n // Q_BLOCK, Q_BLOCK, D_MODEL))
        return out.reshape(n, D_MODEL)
    return hierarchical_moe(hn, *moe_w)


def trunk_layer(x, pos, past, norm_mix_g, w_in, w_br_a, w_br_b, w_out, norm_ffn_g,
                w_rg, b_rg, w_re, b_re, w_eg, w_eu, w_ed):
    b, t, _ = x.shape
    xn = rmsnorm(x, norm_mix_g)
    q_a, k_a, v_a, q_i, k_i, w_i, q_b, k_b, v_b, g_a, g_b = split_in_proj(xn @ w_in)
    q_a = rope_partial(q_a, pos)
    k_a = rope_partial(k_a, pos)
    q_i = rope_partial(q_i, pos)
    k_i = rope_partial(k_i[:, :, None, :], pos)[:, :, 0, :]
    new_rows = (k_a, v_a, k_i, k_b, v_b)
    if past is None:
        ka_all, va_all, ki_all, kb_all, vb_all = new_rows
    else:
        ka_all, va_all, ki_all, kb_all, vb_all = tuple(
            jnp.concatenate([c, n], axis=1) for c, n in zip(past, new_rows))
    topk = min(TOPK_MAX, ka_all.shape[1] // 4)
    out_a = sweep_query_blocks(
        lambda qs, p: dsa_attention(qs[0], qs[1], qs[2], p, ka_all, va_all, ki_all, topk),
        (q_a, q_i, w_i), pos)
    out_b = sweep_query_blocks(
        lambda qs, p: stick_breaking_attention(qs[0], p, kb_all, vb_all), (q_b,), pos)
    y_a = out_a.reshape(b, t, D_A) @ w_br_a
    y_b = out_b.reshape(b, t, D_B) @ w_br_b
    h = x + (jax.nn.sigmoid(g_a) * y_a + jax.nn.sigmoid(g_b) * y_b) @ w_out
    hn = rmsnorm(h, norm_ffn_g).reshape(b * t, D_MODEL)
    ffn = moe_tokens(hn, (w_rg, b_rg, w_re, b_re, w_eg, w_eu, w_ed))
    return h + ffn.reshape(b, t, D_MODEL), new_rows


def setup_inputs(seed: int = 0) -> dict:
    key = jax.random.key(seed)
    ks = jax.random.split(key, 24)

    def nrm(k, shape, scale):
        return jax.random.normal(k, shape, jnp.float32) * scale

    return {
        'x_prompt': nrm(ks[0], (BATCH, SEQ, D_MODEL), 1.0),
        'x_sample': nrm(ks[1], (DEC_BATCH, DEC_SEQ, D_MODEL), 1.0),
        'cache_a_k': nrm(ks[2], (DEPTH, DEC_BATCH, PAST_LEN, N_HEADS_A, HEAD_DIM), 1.0),
        'cache_a_v': nrm(ks[3], (DEPTH, DEC_BATCH, PAST_LEN, N_HEADS_A, HEAD_DIM), 1.0),
        'cache_idx_k': nrm(ks[4], (DEPTH, DEC_BATCH, PAST_LEN, IDX_DIM), 1.0),
        'cache_b_k': nrm(ks[5], (DEPTH, DEC_BATCH, PAST_LEN, N_HEADS_B, HEAD_DIM), 1.0),
        'cache_b_v': nrm(ks[6], (DEPTH, DEC_BATCH, PAST_LEN, N_HEADS_B, HEAD_DIM), 1.0),
        'norm_mix_g': 1.0 + nrm(ks[7], (DEPTH, D_MODEL), 0.1),
        'w_in': nrm(ks[8], (DEPTH, D_MODEL, D_IN), D_MODEL ** -0.5),
        'w_br_a': nrm(ks[9], (DEPTH, D_A, D_MODEL), D_A ** -0.5),
        'w_br_b': nrm(ks[10], (DEPTH, D_B, D_MODEL), D_B ** -0.5),
        'w_out': nrm(ks[11], (DEPTH, D_MODEL, D_MODEL), D_MODEL ** -0.5),
        'norm_ffn_g': 1.0 + nrm(ks[12], (DEPTH, D_MODEL), 0.1),
        'w_router_group': nrm(ks[13], (DEPTH, D_MODEL, N_GROUPS), D_MODEL ** -0.5),
        'b_router_group': nrm(ks[14], (DEPTH, N_GROUPS), 0.01),
        'w_router_expert': nrm(ks[15], (DEPTH, D_MODEL, N_EXPERTS), D_MODEL ** -0.5),
        'b_router_expert': nrm(ks[16], (DEPTH, N_EXPERTS), 0.01),
        'w_exp_gate': nrm(ks[17], (DEPTH, N_EXPERTS, D_MODEL, D_FF_EXPERT), D_MODEL ** -0.5),
        'w_exp_up': nrm(ks[18], (DEPTH, N_EXPERTS, D_MODEL, D_FF_EXPERT), D_MODEL ** -0.5),
        'w_exp_down': nrm(ks[19], (DEPTH, N_EXPERTS, D_FF_EXPERT, D_MODEL), D_FF_EXPERT ** -0.5),
        'norm_final_g': 1.0 + nrm(ks[20], (D_MODEL,), 0.1),
    }


def stack_layers(rows, i):
    return jnp.stack([r[i] for r in rows], axis=0)


def reference(x_prompt, x_sample, cache_a_k, cache_a_v, cache_idx_k, cache_b_k, cache_b_v,
              norm_mix_g, w_in, w_br_a, w_br_b, w_out, norm_ffn_g,
              w_router_group, b_router_group, w_router_expert, b_router_expert,
              w_exp_gate, w_exp_up, w_exp_down, norm_final_g):
    pos_p = jnp.arange(x_prompt.shape[1], dtype=jnp.int32)
    pos_s = PAST_LEN + jnp.arange(x_sample.shape[1], dtype=jnp.int32)
    hp = x_prompt
    hs = x_sample
    rows_p = []
    rows_s = []
    for layer in range(DEPTH):
        lw = (norm_mix_g[layer], w_in[layer], w_br_a[layer], w_br_b[layer], w_out[layer],
              norm_ffn_g[layer], w_router_group[layer], b_router_group[layer],
              w_router_expert[layer], b_router_expert[layer],
              w_exp_gate[layer], w_exp_up[layer], w_exp_down[layer])
        hp, rp = trunk_layer(hp, pos_p, None, *lw)
        past = (cache_a_k[layer], cache_a_v[layer], cache_idx_k[layer], cache_b_k[layer], cache_b_v[layer])
        hs, rs = trunk_layer(hs, pos_s, past, *lw)
        rows_p.append(rp)
        rows_s.append(rs)
    y_prompt = rmsnorm(hp, norm_final_g)
    y_sample = rmsnorm(hs, norm_final_g)
    return (y_prompt, y_sample,
            stack_layers(rows_p, 0), stack_layers(rows_p, 1), stack_layers(rows_p, 2),
            stack_layers(rows_p, 3), stack_layers(rows_p, 4),
            stack_layers(rows_s, 0), stack_layers(rows_s, 1), stack_layers(rows_s, 2),
            stack_layers(rows_s, 3), stack_layers(rows_s, 4))
```

```python
import functools

import numpy as np
import jax
import jax.numpy as jnp
from jax import lax
from jax.experimental import pallas as pl
from jax.experimental.pallas import tpu as pltpu

D_MODEL = 1024
HEAD_DIM = 64
N_HEADS = 8
D_HEADS = N_HEADS * HEAD_DIM
N_IDX_HEADS = 8
IDX_DIM = 64
IDX_SCALE = (N_IDX_HEADS * IDX_DIM) ** -0.5
ATT_SCALE = HEAD_DIM ** -0.5
CHUNK = 64
TOPK_MAX = 256
ROPE_THETA = 500000.0
ROT_DIM = HEAD_DIM // 4
ROT_HALF = ROT_DIM // 2
N_GROUPS = 4
EXPERTS_PER_GROUP = 8
N_EXPERTS = N_GROUPS * EXPERTS_PER_GROUP
D_FF_EXPERT = D_MODEL // 4
RMS_EPS = 1e-6
IN_SPLIT = (D_HEADS, D_HEADS, D_HEADS, N_IDX_HEADS * IDX_DIM, IDX_DIM, N_IDX_HEADS,
            D_HEADS, D_HEADS, D_HEADS, D_MODEL, D_MODEL)

LANES = 128
N_PAIRS = N_HEADS // 2
NEG_BIG = -1e30
SB_DEAD = -110.0
VMEM_LIMIT = 56 * 1024 * 1024

C_QA, C_KA, C_VA, C_QI, C_KI, C_WI, C_QB, C_KB, C_VB, C_GA, C_GB, C_END = (
    0, 512, 1024, 1536, 2048, 2176, 2304, 2816, 3328, 3840, 4864, 5888)

F32 = jnp.float32
BF16 = jnp.bfloat16


def _nt_dot(a, b):
    return lax.dot_general(a, b, (((1,), (1,)), ((), ())), preferred_element_type=F32)


def _head_halves(x):
    lane = lax.broadcasted_iota(jnp.int32, x.shape, 1)
    zero = jnp.zeros_like(x)
    return jnp.where(lane < HEAD_DIM, x, zero), jnp.where(lane >= HEAD_DIM, x, zero)


def _proj_kernel(x_ref, g_ref, w_ref, invf_ref,
                 qa_o, qi_o, qb_o, ka_o, va_o, kb_o, vb_o, ki_o, wi_o,
                 ka16_o, va16_o, kb16_o, vb16_o, ki16_o, sga_o, sgb_o,
                 *, tm, tq, pos0):
    x = x_ref[...]
    r = lax.rsqrt(jnp.mean(x * x, axis=-1, keepdims=True) + RMS_EPS)
    xn = (x * r) * g_ref[...]
    p = jnp.dot(xn.astype(BF16), w_ref[...], preferred_element_type=F32)

    row = pl.program_id(0) * tm + lax.broadcasted_iota(jnp.int32, (tm, 1), 0)
    pos = (row % tq + pos0).astype(F32)
    ang = pos * invf_ref[...]
    c = jnp.cos(ang)
    s = jnp.sin(ang)
    d = lax.broadcasted_iota(jnp.int32, (1, LANES), 1) % HEAD_DIM
    s_lo = jnp.where(d < ROT_HALF, -s, 0.0)
    s_hi = jnp.where((d >= ROT_HALF) & (d < ROT_DIM), s, 0.0)

    def rope(v):
        return (v * c + pltpu.roll(v, LANES - ROT_HALF, 1) * s_lo
                + pltpu.roll(v, ROT_HALF, 1) * s_hi)

    for j in range(D_HEADS // LANES):
        sl = slice(j * LANES, (j + 1) * LANES)
        qa = rope(p[:, C_QA + j * LANES:C_QA + (j + 1) * LANES])
        qa_o[:, sl] = (qa * ATT_SCALE).astype(BF16)
        ka = rope(p[:, C_KA + j * LANES:C_KA + (j + 1) * LANES])
        ka_o[:, sl] = ka
        ka16_o[:, sl] = ka.astype(BF16)
        qi = rope(p[:, C_QI + j * LANES:C_QI + (j + 1) * LANES])
        qi_o[:, sl] = qi.astype(BF16)
    ki = rope(p[:, C_KI:C_KI + LANES])
    ki_o[...] = ki
    ki16_o[...] = ki.astype(BF16)
    wi_o[...] = p[:, C_WI:C_WI + LANES]
    va = p[:, C_VA:C_VA + D_HEADS]
    va_o[...] = va
    va16_o[...] = va.astype(BF16)
    qb_o[...] = (p[:, C_QB:C_QB + D_HEADS] * ATT_SCALE).astype(BF16)
    kb = p[:, C_KB:C_KB + D_HEADS]
    kb_o[...] = kb
    kb16_o[...] = kb.astype(BF16)
    vb = p[:, C_VB:C_VB + D_HEADS]
    vb_o[...] = vb
    vb16_o[...] = vb.astype(BF16)
    sga_o[...] = jax.nn.sigmoid(p[:, C_GA:C_GA + D_MODEL])
    sgb_o[...] = jax.nn.sigmoid(p[:, C_GB:C_GB + D_MODEL])


def _proj(x2d, g, w16, invf, *, tq, pos0):
    n = x2d.shape[0]
    tm = min(256, n)
    assert n % tm == 0
    row = lambda i: (i, 0)
    fix = lambda i: (0, 0)
    widths = dict(qa=D_HEADS, qi=D_HEADS, qb=D_HEADS, ka=D_HEADS, va=D_HEADS, kb=D_HEADS,
                  vb=D_HEADS, ki=LANES, wi=LANES, ka16=D_HEADS, va16=D_HEADS, kb16=D_HEADS,
                  vb16=D_HEADS, ki16=LANES, sga=D_MODEL, sgb=D_MODEL)
    dtypes = dict(qa=BF16, qi=BF16, qb=BF16, ka16=BF16, va16=BF16, kb16=BF16, vb16=BF16, ki16=BF16)
    names = list(widths)
    out_shape = [jax.ShapeDtypeStruct((n, widths[k]), dtypes.get(k, F32)) for k in names]
    out_specs = [pl.BlockSpec((tm, widths[k]), row) for k in names]
    outs = pl.pallas_call(
        functools.partial(_proj_kernel, tm=tm, tq=tq, pos0=pos0),
        grid=(n // tm,),
        in_specs=[pl.BlockSpec((tm, D_MODEL), row),
                  pl.BlockSpec((1, D_MODEL), fix),
                  pl.BlockSpec((D_MODEL, C_END), fix),
                  pl.BlockSpec((1, LANES), fix)],
        out_specs=out_specs,
        out_shape=out_shape,
        compiler_params=pltpu.CompilerParams(dimension_semantics=("arbitrary",),
                                             vmem_limit_bytes=VMEM_LIMIT),
        name="proj",
    )(x2d, g, w16, invf)
    return dict(zip(names, outs))


def _dsa_kernel(qb_of, kt_of, first_of, last_of,
                qa_ref, qi_ref, wi_ref, ki_ref, k_ref, v_ref, o_ref,
                score_ref, thr_ref, jcut_ref, m_ref, l_ref, acc_ref,
                *, qb_rows, kt_rows, ch, pos0, l_valid, topk):
    i = pl.program_id(1)
    qb = qb_of[i]
    kt = kt_of[i]
    n_ch_per_kt = kt_rows // ch

    row = qb * qb_rows + lax.broadcasted_iota(jnp.int32, (qb_rows, 1), 0)
    limit = jnp.minimum(((pos0 + row) // CHUNK + 1) * CHUNK, l_valid)
    max_limit = jnp.minimum(((pos0 + qb * qb_rows + qb_rows - 1) // CHUNK + 1) * CHUNK, l_valid)
    n_kt = (max_limit + kt_rows - 1) // kt_rows
    n_ch = n_kt * n_ch_per_kt

    def count_where(pred_fn):
        def body(t, acc):
            blk = score_ref[t]
            for c in range(ch // LANES):
                acc = acc + jnp.where(pred_fn(t, blk[:, c * LANES:(c + 1) * LANES], c), 1.0, 0.0)
            return acc
        acc = lax.fori_loop(0, n_ch, body, jnp.zeros((qb_rows, LANES), F32))
        return jnp.sum(acc, axis=1, keepdims=True)

    @pl.when(first_of[i] == 1)
    def _select():
        qi = qi_ref[0]
        q_halves = []
        for g in range(N_PAIRS):
            q_halves.extend(_head_halves(qi[:, g * LANES:(g + 1) * LANES]))
        w = wi_ref[0] * IDX_SCALE
        w_cols = [w[:, h:h + 1] for h in range(N_IDX_HEADS)]
        lane = lax.broadcasted_iota(jnp.int32, (1, ch), 1)

        def score_body(t, carry):
            kk = ki_ref[0, pl.ds(pl.multiple_of(t * ch, ch), ch), :]
            s = jnp.zeros((qb_rows, ch), F32)
            for h in range(N_IDX_HEADS):
                s = s + w_cols[h] * jnp.maximum(_nt_dot(q_halves[h], kk), 0.0)
            kpos = t * ch + lane
            score_ref[t] = jnp.where(kpos < limit, s, -jnp.inf)
            return carry
        lax.fori_loop(0, n_ch, score_body, 0)

        def minmax_body(t, carry):
            mn, mx = carry
            blk = score_ref[t]
            for c in range(ch // LANES):
                piece = blk[:, c * LANES:(c + 1) * LANES]
                mx = jnp.maximum(mx, piece)
                mn = jnp.minimum(mn, jnp.where(piece == -jnp.inf, jnp.inf, piece))
            return mn, mx
        mn, mx = lax.fori_loop(0, n_ch, minmax_body,
                               (jnp.full((qb_rows, LANES), jnp.inf, F32),
                                jnp.full((qb_rows, LANES), -jnp.inf, F32)))
        lo0 = jnp.min(mn, axis=1, keepdims=True)
        smax = jnp.max(mx, axis=1, keepdims=True)
        hi0 = smax + (jnp.abs(smax) * 1e-6 + 1e-30)
        all_in = limit <= topk
        done0 = jnp.where(all_in, 1.0, 0.0)
        thr0 = jnp.where(all_in, jnp.float32(-3e38), lo0)
        zeros = jnp.zeros((qb_rows, 1), F32)

        def cond(c):
            return c[0] > 0

        def body(c):
            _, it, lo, hi, chi, thr, done, tie = c
            mid = lo + 0.5 * (hi - lo)
            stuck = (mid <= lo) | (mid >= hi)
            mid_b = jnp.broadcast_to(mid, (qb_rows, LANES))
            cnt = count_where(lambda t, piece, cidx: piece >= mid_b)
            active = done < 0.5
            moving = active & jnp.logical_not(stuck)
            hit = moving & (cnt == topk)
            now_tie = active & stuck
            up = moving & (cnt > topk)
            down = moving & (cnt < topk)
            thr = jnp.where(hit, mid, jnp.where(now_tie, lo, thr))
            tie = jnp.where(now_tie, 1.0, tie)
            done = jnp.where(hit | now_tie, 1.0, done)
            lo = jnp.where(up, mid, lo)
            chi = jnp.where(down, cnt, chi)
            hi = jnp.where(down, mid, hi)
            left = jnp.sum(1.0 - done)
            go = jnp.where((left > 0) & (it < 400), 1, 0)
            return go, it + 1, lo, hi, chi, thr, done, tie

        left0 = jnp.sum(1.0 - done0)
        state = (jnp.where(left0 > 0, 1, 0), jnp.int32(0), lo0, hi0, zeros, thr0, done0, zeros)
        _, _, _, _, chi, thr, _, tie = lax.while_loop(cond, body, state)

        big = jnp.full((qb_rows, 1), 2 ** 30, jnp.int32)

        def tie_break():
            need = topk - chi
            thr_b = jnp.broadcast_to(thr, (qb_rows, LANES))
            lane1 = lax.broadcasted_iota(jnp.int32, (1, LANES), 1)

            def tb_body(_, c):
                lo_j, hi_j = c
                mid_j = lo_j + (hi_j - lo_j) // 2
                mid_jb = jnp.broadcast_to(mid_j, (qb_rows, LANES))
                cnt = count_where(lambda t, piece, cidx: (piece == thr_b)
                                  & (t * ch + cidx * LANES + lane1 <= mid_jb))
                ok = cnt >= need
                return jnp.where(ok, lo_j, mid_j), jnp.where(ok, mid_j, hi_j)

            lo_j = jnp.full((qb_rows, 1), -1, jnp.int32)
            hi_j = jnp.zeros((qb_rows, 1), jnp.int32) + (n_ch * ch - 1)
            n_steps = int(np.ceil(np.log2(score_ref.shape[0] * ch))) + 1
            _, hi_j = lax.fori_loop(0, n_steps, tb_body, (lo_j, hi_j))
            return jnp.where(tie > 0, hi_j, big)

        jcut = lax.cond(jnp.sum(tie) > 0, tie_break, lambda: big)
        thr_ref[...] = jnp.broadcast_to(thr, (qb_rows, LANES))
        jcut_ref[...] = jnp.broadcast_to(jcut, (qb_rows, LANES))
        m_ref[...] = jnp.full(m_ref.shape, NEG_BIG, F32)
        l_ref[...] = jnp.zeros(l_ref.shape, F32)
        acc_ref[...] = jnp.zeros(acc_ref.shape, F32)

    thr_b = thr_ref[...]
    jcut_b = jcut_ref[...]
    lane1 = lax.broadcasted_iota(jnp.int32, (1, LANES), 1)
    bias = []
    for cc in range(n_ch_per_kt):
        blk = score_ref[kt * n_ch_per_kt + cc]
        for c in range(ch // LANES):
            piece = blk[:, c * LANES:(c + 1) * LANES]
            kpos = kt * kt_rows + cc * ch + c * LANES + lane1
            sel = (piece >= thr_b) & ((piece > thr_b) | (kpos <= jcut_b))
            bias.append(jnp.where(sel, 0.0, NEG_BIG))
    n_piece = len(bias)

    qa = qa_ref[0]
    for g in range(N_PAIRS):
        k_g = k_ref[0, :, g * LANES:(g + 1) * LANES]
        v_g = v_ref[0, :, g * LANES:(g + 1) * LANES]
        for half, q_h in enumerate(_head_halves(qa[:, g * LANES:(g + 1) * LANES])):
            h = 2 * g + half
            logits = _nt_dot(q_h, k_g)
            pieces = [logits[:, c * LANES:(c + 1) * LANES] + bias[c] for c in range(n_piece)]
            mx = pieces[0]
            for c in range(1, n_piece):
                mx = jnp.maximum(mx, pieces[c])
            m_old = m_ref[h]
            m_new = jnp.maximum(m_old, jnp.max(mx, axis=1, keepdims=True))
            alpha = jnp.exp(m_old - m_new)
            ps = [jnp.exp(pc - m_new) for pc in pieces]
            psum = ps[0]
            for c in range(1, n_piece):
                psum = psum + ps[c]
            p16 = jnp.concatenate([pc.astype(BF16) for pc in ps], axis=1)
            pv = jnp.dot(p16, v_g, preferred_element_type=F32)
            m_ref[h] = m_new
            l_ref[h] = alpha * l_ref[h] + psum
            acc_ref[h] = alpha * acc_ref[h] + pv

    @pl.when(last_of[i] == 1)
    def _finish():
        lane = lax.broadcasted_iota(jnp.int32, (qb_rows, LANES), 1)
        for g in range(N_PAIRS):
            outs = []
            for half in range(2):
                h = 2 * g + half
                denom = jnp.sum(l_ref[h], axis=1, keepdims=True)
                outs.append(acc_ref[h] / denom)
            o_ref[0, :, g * LANES:(g + 1) * LANES] = jnp.where(
                lane < HEAD_DIM, outs[0], outs[1]).astype(o_ref.dtype)


def _dsa(qa, qi, wi, ki16, k16, v16, *, pos0, l_valid, qb_rows, kt_rows, ch):
    bsz, tq, _ = qa.shape
    l_pad = k16.shape[1]
    assert tq % qb_rows == 0 and l_pad % kt_rows == 0 and kt_rows % ch == 0
    topk = min(TOPK_MAX, l_valid // 4)
    qb_l, kt_l, first_l, last_l = [], [], [], []
    for qb in range(tq // qb_rows):
        max_limit = min(((pos0 + qb * qb_rows + qb_rows - 1) // CHUNK + 1) * CHUNK, l_valid)
        n_kt = -(-max_limit // kt_rows)
        for kt in range(n_kt):
            qb_l.append(qb)
            kt_l.append(kt)
            first_l.append(int(kt == 0))
            last_l.append(int(kt == n_kt - 1))
    tables = [jnp.asarray(np.asarray(a, np.int32)) for a in (qb_l, kt_l, first_l, last_l)]
    q_map = lambda b, i, qb_of, kt_of, f, l: (b, qb_of[i], 0)
    kv_map = lambda b, i, qb_of, kt_of, f, l: (b, kt_of[i], 0)
    all_map = lambda b, i, qb_of, kt_of, f, l: (b, 0, 0)
    grid_spec = pltpu.PrefetchScalarGridSpec(
        num_scalar_prefetch=4,
        grid=(bsz, len(qb_l)),
        in_specs=[pl.BlockSpec((1, qb_rows, D_HEADS), q_map),
                  pl.BlockSpec((1, qb_rows, D_HEADS), q_map),
                  pl.BlockSpec((1, qb_rows, LANES), q_map),
                  pl.BlockSpec((1, l_pad, LANES), all_map),
                  pl.BlockSpec((1, kt_rows, D_HEADS), kv_map),
                  pl.BlockSpec((1, kt_rows, D_HEADS), kv_map)],
        out_specs=pl.BlockSpec((1, qb_rows, D_HEADS), q_map),
        scratch_shapes=[pltpu.VMEM((l_pad // ch, qb_rows, ch), F32),
                        pltpu.VMEM((qb_rows, LANES), F32),
                        pltpu.VMEM((qb_rows, LANES), jnp.int32),
                        pltpu.VMEM((N_HEADS, qb_rows, LANES), F32),
                        pltpu.VMEM((N_HEADS, qb_rows, LANES), F32),
                        pltpu.VMEM((N_HEADS, qb_rows, LANES), F32)])
    return pl.pallas_call(
        functools.partial(_dsa_kernel, qb_rows=qb_rows, kt_rows=kt_rows, ch=ch, pos0=pos0,
                          l_valid=l_valid, topk=topk),
        grid_spec=grid_spec,
        out_shape=jax.ShapeDtypeStruct((bsz, tq, D_HEADS), BF16),
        compiler_params=pltpu.CompilerParams(dimension_semantics=("arbitrary", "arbitrary"),
                                             vmem_limit_bytes=VMEM_LIMIT),
        name="dsa",
    )(*tables, qa, qi, wi, ki16, k16, v16)


def _sb_kernel(q_ref, k_ref, v_ref, o_ref, *, qb_rows, pos0):
    qb = pl.program_id(1)
    qpos = pos0 + qb * qb_rows + lax.broadcasted_iota(jnp.int32, (qb_rows, 1), 0)
    kt_top = (pos0 + qb * qb_rows + qb_rows - 1) // LANES
    lane1 = lax.broadcasted_iota(jnp.int32, (1, LANES), 1)
    jj = lax.broadcasted_iota(jnp.int32, (LANES, LANES), 0)
    ss = lax.broadcasted_iota(jnp.int32, (LANES, LANES), 1)
    after = jnp.where(jj > ss, 1.0, 0.0).astype(BF16)
    ones = jnp.ones((LANES, LANES), BF16)
    lane = lax.broadcasted_iota(jnp.int32, (qb_rows, LANES), 1)

    def split_dot(a, b):
        a_hi = a.astype(BF16)
        a_lo = (a - a_hi.astype(F32)).astype(BF16)
        return (jnp.dot(a_hi, b, preferred_element_type=F32)
                + jnp.dot(a_lo, b, preferred_element_type=F32))

    q = q_ref[0]
    for g in range(N_PAIRS):
        outs = []
        for q_h in _head_halves(q[:, g * LANES:(g + 1) * LANES]):
            def cond(c):
                return c[0] > 0

            def body(c, q_h=q_h, g=g):
                _, kt, run, acc = c
                rows = pl.ds(pl.multiple_of(kt * LANES, LANES), LANES)
                k_t = k_ref[0, rows, g * LANES:(g + 1) * LANES]
                v_t = v_ref[0, rows, g * LANES:(g + 1) * LANES]
                z = _nt_dot(q_h, k_t)
                strict = (kt * LANES + lane1) < qpos
                sp = jnp.maximum(z, 0.0) + jnp.log1p(jnp.exp(-jnp.abs(z)))
                ls = jnp.where(strict, -sp, 0.0)
                log_rest = run + split_dot(ls, after)
                wgt = jnp.where(strict, jnp.exp(z - sp + log_rest), 0.0)
                acc = acc + jnp.dot(wgt.astype(BF16), v_t, preferred_element_type=F32)
                run = run + split_dot(ls, ones)
                go = jnp.where((kt > 0) & (jnp.max(run) > SB_DEAD), 1, 0)
                return go, kt - 1, run, acc

            zero = jnp.zeros((qb_rows, LANES), F32)
            _, _, _, acc = lax.while_loop(cond, body, (jnp.int32(1), kt_top, zero, zero))
            outs.append(acc)
        o_ref[0, :, g * LANES:(g + 1) * LANES] = jnp.where(
            lane < HEAD_DIM, outs[0], outs[1]).astype(o_ref.dtype)


def _sb(qb16, k16, v16, *, pos0, qb_rows):
    bsz, tq, _ = qb16.shape
    l_pad = k16.shape[1]
    assert tq % qb_rows == 0 and l_pad % LANES == 0
    q_map = lambda b, i: (b, i, 0)
    all_map = lambda b, i: (b, 0, 0)
    resident = dict(pipeline_mode=pl.Buffered(1)) if bsz == 1 else {}
    return pl.pallas_call(
        functools.partial(_sb_kernel, qb_rows=qb_rows, pos0=pos0),
        grid=(bsz, tq // qb_rows),
        in_specs=[pl.BlockSpec((1, qb_rows, D_HEADS), q_map),
                  pl.BlockSpec((1, l_pad, D_HEADS), all_map, **resident),
                  pl.BlockSpec((1, l_pad, D_HEADS), all_map, **resident)],
        out_specs=pl.BlockSpec((1, qb_rows, D_HEADS), q_map),
        out_shape=jax.ShapeDtypeStruct((bsz, tq, D_HEADS), BF16),
        compiler_params=pltpu.CompilerParams(dimension_semantics=("arbitrary", "arbitrary"),
                                             vmem_limit_bytes=VMEM_LIMIT),
        name="sb",
    )(qb16, k16, v16)


def _post_kernel(oa_ref, ob_ref, sga_ref, sgb_ref, x_ref, wa_ref, wb_ref, wo_ref, g_ref,
                 wr_hi_ref, wr_lo_ref, br_ref, h_o, hn_o, comb_o, *, tm):
    ya = jnp.dot(oa_ref[...], wa_ref[...], preferred_element_type=F32)
    yb = jnp.dot(ob_ref[...], wb_ref[...], preferred_element_type=F32)
    mix = sga_ref[...] * ya + sgb_ref[...] * yb
    h = x_ref[...] + jnp.dot(mix.astype(BF16), wo_ref[...], preferred_element_type=F32)
    h_o[...] = h
    r = lax.rsqrt(jnp.mean(h * h, axis=-1, keepdims=True) + RMS_EPS)
    hn = (h * r) * g_ref[...]
    hn_hi = hn.astype(BF16)
    hn_o[...] = hn_hi
    hn_lo = (hn - hn_hi.astype(F32)).astype(BF16)
    logits = (jnp.dot(hn_hi, wr_hi_ref[...], preferred_element_type=F32)
              + jnp.dot(hn_lo, wr_hi_ref[...], preferred_element_type=F32)
              + jnp.dot(hn_hi, wr_lo_ref[...], preferred_element_type=F32)) + br_ref[...]

    lane = lax.broadcasted_iota(jnp.int32, (tm, LANES), 1)
    is_g = (lane >= N_EXPERTS) & (lane < N_EXPERTS + N_GROUPS)
    gl = jnp.where(is_g, logits, -jnp.inf)
    gmax = jnp.max(gl, axis=1, keepdims=True)
    g_lane = jnp.min(jnp.where(gl == gmax, lane, 2 ** 30), axis=1, keepdims=True)
    g_w = 1.0 / jnp.sum(jnp.exp(gl - gmax), axis=1, keepdims=True)
    in_grp = (lane < N_EXPERTS) & ((lane // EXPERTS_PER_GROUP) == (g_lane - N_EXPERTS))
    e1 = jnp.where(in_grp, logits, -jnp.inf)
    v1 = jnp.max(e1, axis=1, keepdims=True)
    i1 = jnp.min(jnp.where(e1 == v1, lane, 2 ** 30), axis=1, keepdims=True)
    e2 = jnp.where(lane == i1, -jnp.inf, e1)
    v2 = jnp.max(e2, axis=1, keepdims=True)
    i2 = jnp.min(jnp.where(e2 == v2, lane, 2 ** 30), axis=1, keepdims=True)
    t2 = jnp.exp(v2 - v1)
    den = 1.0 + t2
    comb_o[...] = jnp.where(lane == i1, (1.0 / den) * g_w,
                            jnp.where(lane == i2, (t2 / den) * g_w, 0.0))


def _post(oa, ob, sga, sgb, x2d, wa16, wb16, wo16, g_ffn, wr_hi, wr_lo, br):
    n = x2d.shape[0]
    tm = min(256, n)
    assert n % tm == 0
    row = lambda i: (i, 0)
    fix = lambda i: (0, 0)
    return pl.pallas_call(
        functools.partial(_post_kernel, tm=tm),
        grid=(n // tm,),
        in_specs=[pl.BlockSpec((tm, D_HEADS), row), pl.BlockSpec((tm, D_HEADS), row),
                  pl.BlockSpec((tm, D_MODEL), row), pl.BlockSpec((tm, D_MODEL), row),
                  pl.BlockSpec((tm, D_MODEL), row),
                  pl.BlockSpec((D_HEADS, D_MODEL), fix), pl.BlockSpec((D_HEADS, D_MODEL), fix),
                  pl.BlockSpec((D_MODEL, D_MODEL), fix), pl.BlockSpec((1, D_MODEL), fix),
                  pl.BlockSpec((D_MODEL, LANES), fix), pl.BlockSpec((D_MODEL, LANES), fix),
                  pl.BlockSpec((1, LANES), fix)],
        out_specs=[pl.BlockSpec((tm, D_MODEL), row), pl.BlockSpec((tm, D_MODEL), row),
                   pl.BlockSpec((tm, LANES), row)],
        out_shape=[jax.ShapeDtypeStruct((n, D_MODEL), F32),
                   jax.ShapeDtypeStruct((n, D_MODEL), BF16),
                   jax.ShapeDtypeStruct((n, LANES), F32)],
        compiler_params=pltpu.CompilerParams(dimension_semantics=("arbitrary",),
                                             vmem_limit_bytes=VMEM_LIMIT),
        name="post",
    )(oa, ob, sga, sgb, x2d, wa16, wb16, wo16, g_ffn, wr_hi, wr_lo, br)


def _moe_kernel(hn_ref, comb_ref, h_ref, wg_ref, wu_ref, wd_ref, gf_ref, y_o, acc_ref, *, tm):
    e = pl.program_id(1)

    @pl.when(e == 0)
    def _init():
        acc_ref[...] = jnp.zeros(acc_ref.shape, F32)

    hn = hn_ref[...]
    gate = jnp.dot(hn, wg_ref[0], preferred_element_type=F32)
    up = jnp.dot(hn, wu_ref[0], preferred_element_type=F32)
    lane = lax.broadcasted_iota(jnp.int32, (tm, LANES), 1)
    c_e = jnp.sum(jnp.where(lane == e, comb_ref[...], 0.0), axis=1, keepdims=True)
    hid = (gate * jax.nn.sigmoid(gate)) * up * c_e
    acc_ref[...] += jnp.dot(hid.astype(BF16), wd_ref[0], preferred_element_type=F32)

    @pl.when(e == N_EXPERTS - 1)
    def _finish():
        out = h_ref[...] + acc_ref[...]
        r = lax.rsqrt(jnp.mean(out * out, axis=-1, keepdims=True) + RMS_EPS)
        y_o[...] = (out * r) * gf_ref[...]


def _moe(hn16, comb, h, wg16, wu16, wd16, g_final):
    n = h.shape[0]
    tm = min(1024, n)
    assert n % tm == 0
    row = lambda i, e: (i, 0)
    exp = lambda i, e: (e, 0, 0)
    fix = lambda i, e: (0, 0)
    return pl.pallas_call(
        functools.partial(_moe_kernel, tm=tm),
        grid=(n // tm, N_EXPERTS),
        in_specs=[pl.BlockSpec((tm, D_MODEL), row), pl.BlockSpec((tm, LANES), row),
                  pl.BlockSpec((tm, D_MODEL), row),
                  pl.BlockSpec((1, D_MODEL, D_FF_EXPERT), exp),
                  pl.BlockSpec((1, D_MODEL, D_FF_EXPERT), exp),
                  pl.BlockSpec((1, D_FF_EXPERT, D_MODEL), exp),
                  pl.BlockSpec((1, D_MODEL), fix)],
        out_specs=pl.BlockSpec((tm, D_MODEL), row),
        out_shape=jax.ShapeDtypeStruct((n, D_MODEL), F32),
        scratch_shapes=[pltpu.VMEM((tm, D_MODEL), F32)],
        compiler_params=pltpu.CompilerParams(dimension_semantics=("arbitrary", "arbitrary"),
                                             vmem_limit_bytes=VMEM_LIMIT),
        name="moe",
    )(hn16, comb, h, wg16, wu16, wd16, g_final)


def _prep_weights(norm_mix_g, w_in, w_br_a, w_br_b, w_out, norm_ffn_g, w_rg, b_rg, w_re, b_re,
                  w_eg, w_eu, w_ed, norm_final_g):
    offs = np.cumsum(IN_SPLIT)[:-1].tolist()
    q_a, k_a, v_a, q_i, k_i, w_i, q_b, k_b, v_b, g_a, g_b = jnp.split(w_in, offs, axis=-1)
    w_i_pad = jnp.pad(w_i, ((0, 0), (0, LANES - N_IDX_HEADS)))
    w16 = jnp.concatenate([q_a, k_a, v_a, q_i, k_i, k_i, w_i_pad, q_b, k_b, v_b, g_a, g_b],
                          axis=1).astype(BF16)
    half = ROT_HALF
    inv_freq = ROPE_THETA ** (-jnp.arange(half, dtype=F32) / half)
    d = np.arange(LANES) % HEAD_DIM
    invf = jnp.where(jnp.asarray(d < ROT_DIM), inv_freq[jnp.asarray(d % half)], 0.0)[None, :]
    pad = LANES - N_EXPERTS - N_GROUPS
    w_r = jnp.pad(jnp.concatenate([w_re, w_rg], axis=1), ((0, 0), (0, pad)))
    wr_hi = w_r.astype(BF16)
    wr_lo = (w_r - wr_hi.astype(F32)).astype(BF16)
    b_r = jnp.pad(jnp.concatenate([b_re, b_rg]), (0, pad))[None, :]
    return dict(g_mix=norm_mix_g[None, :], w16=w16, invf=invf,
                wa16=w_br_a.astype(BF16), wb16=w_br_b.astype(BF16), wo16=w_out.astype(BF16),
                g_ffn=norm_ffn_g[None, :], wr_hi=wr_hi, wr_lo=wr_lo, b_r=b_r,
                wg16=w_eg.astype(BF16), wu16=w_eu.astype(BF16), wd16=w_ed.astype(BF16),
                g_final=norm_final_g[None, :])


def _layer(x, pos0, past, wts, *, dsa_qb, dsa_kt, dsa_ch, sb_qb):
    bsz, t, _ = x.shape
    n = bsz * t
    x2d = x.reshape(n, D_MODEL)
    pr = _proj(x2d, wts["g_mix"], wts["w16"], wts["invf"], tq=t, pos0=pos0)
    shp = lambda a: a.reshape(bsz, t, a.shape[-1])

    if past is None:
        l_valid = t
        ka16, va16, ki16, kb16, vb16 = (shp(pr[k]) for k in ("ka16", "va16", "ki16", "kb16", "vb16"))
    else:
        p_len = past[0].shape[1]
        l_valid = p_len + t
        l_pad = -(-l_valid // dsa_kt) * dsa_kt

        def join(cache2d, new16):
            zeros = jnp.zeros((bsz, l_pad - l_valid, new16.shape[-1]), BF16)
            return jnp.concatenate([cache2d.astype(BF16), shp(new16), zeros], axis=1)

        flat = lambda c: c.reshape(bsz, p_len, -1)
        ka16 = join(flat(past[0]), pr["ka16"])
        va16 = join(flat(past[1]), pr["va16"])
        ki_c = flat(past[2])
        ki16 = join(jnp.concatenate([ki_c, ki_c], axis=-1), pr["ki16"])
        kb16 = join(flat(past[3]), pr["kb16"])
        vb16 = join(flat(past[4]), pr["vb16"])

    oa = _dsa(shp(pr["qa"]), shp(pr["qi"]), shp(pr["wi"]), ki16, ka16, va16, pos0=pos0,
              l_valid=l_valid, qb_rows=dsa_qb, kt_rows=dsa_kt, ch=dsa_ch)
    ob = _sb(shp(pr["qb"]), kb16, vb16, pos0=pos0, qb_rows=sb_qb)
    h, hn16, comb = _post(oa.reshape(n, D_HEADS), ob.reshape(n, D_HEADS), pr["sga"], pr["sgb"],
                          x2d, wts["wa16"], wts["wb16"], wts["wo16"], wts["g_ffn"],
                          wts["wr_hi"], wts["wr_lo"], wts["b_r"])
    y = _moe(hn16, comb, h, wts["wg16"], wts["wu16"], wts["wd16"], wts["g_final"])
    heads = lambda a: a.reshape(1, bsz, t, N_HEADS, HEAD_DIM)
    rows = (heads(pr["ka"]), heads(pr["va"]),
            pr["ki"][:, :IDX_DIM].reshape(1, bsz, t, IDX_DIM),
            heads(pr["kb"]), heads(pr["vb"]))
    return y.reshape(bsz, t, D_MODEL), rows


def kernel(x_prompt, x_sample, cache_a_k, cache_a_v, cache_idx_k, cache_b_k, cache_b_v,
           norm_mix_g, w_in, w_br_a, w_br_b, w_out, norm_ffn_g,
           w_router_group, b_router_group, w_router_expert, b_router_expert,
           w_exp_gate, w_exp_up, w_exp_down, norm_final_g):
    assert w_in.shape[0] == 1, "single-layer model"
    wts = _prep_weights(norm_mix_g[0], w_in[0], w_br_a[0], w_br_b[0], w_out[0], norm_ffn_g[0],
                        w_router_group[0], b_router_group[0], w_router_expert[0],
                        b_router_expert[0], w_exp_gate[0], w_exp_up[0], w_exp_down[0],
                        norm_final_g)
    y_p, rows_p = _layer(x_prompt, 0, None, wts, dsa_qb=256, dsa_kt=1024, dsa_ch=512, sb_qb=128)
    past = (cache_a_k[0], cache_a_v[0], cache_idx_k[0], cache_b_k[0], cache_b_v[0])
    p_len = cache_a_k.shape[2]
    t_s = x_sample.shape[1]
    l_s = -(-(p_len + t_s) // 512) * 512
    y_s, rows_s = _layer(x_sample, p_len, past, wts, dsa_qb=t_s, dsa_kt=l_s, dsa_ch=512, sb_qb=t_s)
    return (y_p, y_s) + rows_p + rows_s
```

```python
import functools

import numpy as np
import jax
import jax.numpy as jnp
from jax import lax
from jax.experimental import pallas as pl
from jax.experimental.pallas import tpu as pltpu

D_MODEL = 1024
HEAD_DIM = 64
N_HEADS = 8
D_HEADS = N_HEADS * HEAD_DIM
N_IDX_HEADS = 8
IDX_DIM = 64
IDX_SCALE = (N_IDX_HEADS * IDX_DIM) ** -0.5
ATT_SCALE = HEAD_DIM ** -0.5
CHUNK = 64
TOPK_MAX = 256
ROPE_THETA = 500000.0
ROT_DIM = HEAD_DIM // 4
ROT_HALF = ROT_DIM // 2
N_GROUPS = 4
EXPERTS_PER_GROUP = 8
N_EXPERTS = N_GROUPS * EXPERTS_PER_GROUP
D_FF_EXPERT = D_MODEL // 4
RMS_EPS = 1e-6
IN_SPLIT = (D_HEADS, D_HEADS, D_HEADS, N_IDX_HEADS * IDX_DIM, IDX_DIM, N_IDX_HEADS,
            D_HEADS, D_HEADS, D_HEADS, D_MODEL, D_MODEL)

LANES = 128
N_PAIRS = N_HEADS // 2
NEG_BIG = -1e30
SB_DEAD = -110.0
VMEM_LIMIT = 56 * 1024 * 1024

C_QA, C_KA, C_VA, C_QI, C_KI, C_WI, C_QB, C_KB, C_VB, C_GA, C_GB, C_END = (
    0, 512, 1024, 1536, 2048, 2176, 2304, 2816, 3328, 3840, 4864, 5888)

F32 = jnp.float32
BF16 = jnp.bfloat16


def _nt_dot(a, b):
    return lax.dot_general(a, b, (((1,), (1,)), ((), ())), preferred_element_type=F32)


def _head_halves(x):
    lane = lax.broadcasted_iota(jnp.int32, x.shape, 1)
    zero = jnp.zeros_like(x)
    return jnp.where(lane < HEAD_DIM, x, zero), jnp.where(lane >= HEAD_DIM, x, zero)


def _proj_kernel(x_ref, g_ref, w_ref, invf_ref,
                 qa_o, qi_o, qb_o, ka_o, va_o, kb_o, vb_o, ki_o, wi_o,
                 ka16_o, va16_o, kb16_o, vb16_o, ki16_o, sga_o, sgb_o,
                 *, tm, tq, pos0):
    x = x_ref[...]
    r = lax.rsqrt(jnp.mean(x * x, axis=-1, keepdims=True) + RMS_EPS)
    xn = (x * r) * g_ref[...]
    p = jnp.dot(xn.astype(BF16), w_ref[...], preferred_element_type=F32)

    row = pl.program_id(0) * tm + lax.broadcasted_iota(jnp.int32, (tm, 1), 0)
    pos = (row % tq + pos0).astype(F32)
    ang = pos * invf_ref[...]
    c = jnp.cos(ang)
    s = jnp.sin(ang)
    d = lax.broadcasted_iota(jnp.int32, (1, LANES), 1) % HEAD_DIM
    s_lo = jnp.where(d < ROT_HALF, -s, 0.0)
    s_hi = jnp.where((d >= ROT_HALF) & (d < ROT_DIM), s, 0.0)

    def rope(v):
        return (v * c + pltpu.roll(v, LANES - ROT_HALF, 1) * s_lo
                + pltpu.roll(v, ROT_HALF, 1) * s_hi)

    for j in range(D_HEADS // LANES):
        sl = slice(j * LANES, (j + 1) * LANES)
        qa = rope(p[:, C_QA + j * LANES:C_QA + (j + 1) * LANES])
        qa_o[:, sl] = (qa * ATT_SCALE).astype(BF16)
        ka = rope(p[:, C_KA + j * LANES:C_KA + (j + 1) * LANES])
        ka_o[:, sl] = ka
        ka16_o[:, sl] = ka.astype(BF16)
        qi = rope(p[:, C_QI + j * LANES:C_QI + (j + 1) * LANES])
        qi_o[:, sl] = qi.astype(BF16)
    ki = rope(p[:, C_KI:C_KI + LANES])
    ki_o[...] = ki
    ki16_o[...] = ki.astype(BF16)
    wi_o[...] = p[:, C_WI:C_WI + LANES]
    va = p[:, C_VA:C_VA + D_HEADS]
    va_o[...] = va
    va16_o[...] = va.astype(BF16)
    qb_o[...] = (p[:, C_QB:C_QB + D_HEADS] * ATT_SCALE).astype(BF16)
    kb = p[:, C_KB:C_KB + D_HEADS]
    kb_o[...] = kb
    kb16_o[...] = kb.astype(BF16)
    vb = p[:, C_VB:C_VB + D_HEADS]
    vb_o[...] = vb
    vb16_o[...] = vb.astype(BF16)
    sga_o[...] = jax.nn.sigmoid(p[:, C_GA:C_GA + D_MODEL])
    sgb_o[...] = jax.nn.sigmoid(p[:, C_GB:C_GB + D_MODEL])


def _proj(x2d, g, w16, invf, *, tq, pos0):
    n = x2d.shape[0]
    tm = min(256, n)
    assert n % tm == 0
    row = lambda i: (i, 0)
    fix = lambda i: (0, 0)
    widths = dict(qa=D_HEADS, qi=D_HEADS, qb=D_HEADS, ka=D_HEADS, va=D_HEADS, kb=D_HEADS,
                  vb=D_HEADS, ki=LANES, wi=LANES, ka16=D_HEADS, va16=D_HEADS, kb16=D_HEADS,
                  vb16=D_HEADS, ki16=LANES, sga=D_MODEL, sgb=D_MODEL)
    dtypes = dict(qa=BF16, qi=BF16, qb=BF16, ka16=BF16, va16=BF16, kb16=BF16, vb16=BF16, ki16=BF16)
    names = list(widths)
    out_shape = [jax.ShapeDtypeStruct((n, widths[k]), dtypes.get(k, F32)) for k in names]
    out_specs = [pl.BlockSpec((tm, widths[k]), row) for k in names]
    outs = pl.pallas_call(
        functools.partial(_proj_kernel, tm=tm, tq=tq, pos0=pos0),
        grid=(n // tm,),
        in_specs=[pl.BlockSpec((tm, D_MODEL), row),
                  pl.BlockSpec((1, D_MODEL), fix),
                  pl.BlockSpec((D_MODEL, C_END), fix),
                  pl.BlockSpec((1, LANES), fix)],
        out_specs=out_specs,
        out_shape=out_shape,
        compiler_params=pltpu.CompilerParams(dimension_semantics=("arbitrary",),
                                             vmem_limit_bytes=VMEM_LIMIT),
        name="proj",
    )(x2d, g, w16, invf)
    return dict(zip(names, outs))


def _dsa_kernel(qb_of, kt_of, first_of, last_of,
                qa_ref, qi_ref, wi_ref, ki_ref, k_ref, v_ref, o_ref,
                score_ref, thr_ref, jcut_ref, m_ref, l_ref, acc_ref,
                *, qb_rows, kt_rows, ch, pos0, l_valid, topk):
    i = pl.program_id(1)
    qb = qb_of[i]
    kt = kt_of[i]
    n_ch_per_kt = kt_rows // ch

    row = qb * qb_rows + lax.broadcasted_iota(jnp.int32, (qb_rows, 1), 0)
    limit = jnp.minimum(((pos0 + row) // CHUNK + 1) * CHUNK, l_valid)
    max_limit = jnp.minimum(((pos0 + qb * qb_rows + qb_rows - 1) // CHUNK + 1) * CHUNK, l_valid)
    n_kt = (max_limit + kt_rows - 1) // kt_rows
    n_ch = n_kt * n_ch_per_kt

    def count_where(pred_fn):
        def body(t, acc):
            blk = score_ref[t]
            for c in range(ch // LANES):
                acc = acc + jnp.where(pred_fn(t, blk[:, c * LANES:(c + 1) * LANES], c), 1.0, 0.0)
            return acc
        acc = lax.fori_loop(0, n_ch, body, jnp.zeros((qb_rows, LANES), F32))
        return jnp.sum(acc, axis=1, keepdims=True)

    @pl.when(first_of[i] == 1)
    def _select():
        qi = qi_ref[0]
        q_halves = []
        for g in range(N_PAIRS):
            q_halves.extend(_head_halves(qi[:, g * LANES:(g + 1) * LANES]))
        w = wi_ref[0] * IDX_SCALE
        w_cols = [w[:, h:h + 1] for h in range(N_IDX_HEADS)]
        lane = lax.broadcasted_iota(jnp.int32, (1, ch), 1)

        def score_body(t, carry):
            kk = ki_ref[0, pl.ds(pl.multiple_of(t * ch, ch), ch), :]
            s = jnp.zeros((qb_rows, ch), F32)
            for h in range(N_IDX_HEADS):
                s = s + w_cols[h] * jnp.maximum(_nt_dot(q_halves[h], kk), 0.0)
            kpos = t * ch + lane
            score_ref[t] = jnp.where(kpos < limit, s, -jnp.inf)
            return carry
        lax.fori_loop(0, n_ch, score_body, 0)

        def minmax_body(t, carry):
            mn, mx = carry
            blk = score_ref[t]
            for c in range(ch // LANES):
                piece = blk[:, c * LANES:(c + 1) * LANES]
                mx = jnp.maximum(mx, piece)
                mn = jnp.minimum(mn, jnp.where(piece == -jnp.inf, jnp.inf, piece))
            return mn, mx
        mn, mx = lax.fori_loop(0, n_ch, minmax_body,
                               (jnp.full((qb_rows, LANES), jnp.inf, F32),
                                jnp.full((qb_rows, LANES), -jnp.inf, F32)))
        lo0 = jnp.min(mn, axis=1, keepdims=True)
        smax = jnp.max(mx, axis=1, keepdims=True)
        hi0 = smax + (jnp.abs(smax) * 1e-6 + 1e-30)
        all_in = limit <= topk
        done0 = jnp.where(all_in, 1.0, 0.0)
        thr0 = jnp.where(all_in, jnp.float32(-3e38), lo0)
        zeros = jnp.zeros((qb_rows, 1), F32)

        def cond(c):
            return c[0] > 0

        def bisect_pass(lo, hi, chi, thr, done, tie, cand, pend):
            half = lo + 0.5 * (hi - lo)
            has_cand = pend > 0.5
            mid = jnp.where(has_cand, cand, half)
            stuck = jnp.logical_not(has_cand) & ((half <= lo) | (half >= hi))
            mid_b = jnp.broadcast_to(mid, (qb_rows, LANES))
            cnt = count_where(lambda t, piece, cidx: piece >= mid_b)
            active = done < 0.5
            moving = active & jnp.logical_not(stuck)
            found = moving & ((cnt == topk) | (has_cand & (cnt > topk)))
            now_tie = (active & stuck) | (moving & has_cand & (cnt > topk))
            up = moving & jnp.logical_not(has_cand) & (cnt > topk)
            down = moving & (cnt < topk)
            thr = jnp.where(found, mid, jnp.where(active & stuck, lo, thr))
            tie = jnp.where(now_tie, 1.0, tie)
            done = jnp.where(found | now_tie, 1.0, done)
            lo = jnp.where(up, mid, lo)
            chi = jnp.where(down, cnt, chi)
            hi = jnp.where(down, mid, hi)
            return lo, hi, chi, thr, done, tie, cand, jnp.zeros_like(pend)

        def snap_pass(lo, hi, chi, thr, done, tie, cand, pend):
            hi_b = jnp.broadcast_to(hi, (qb_rows, LANES))

            def body(t, acc):
                blk = score_ref[t]
                for c in range(ch // LANES):
                    piece = blk[:, c * LANES:(c + 1) * LANES]
                    acc = jnp.maximum(acc, jnp.where(piece < hi_b, piece, -jnp.inf))
                return acc
            acc = lax.fori_loop(0, n_ch, body, jnp.full((qb_rows, LANES), -jnp.inf, F32))
            cand = jnp.max(acc, axis=1, keepdims=True)
            return lo, hi, chi, thr, done, tie, cand, 1.0 - done

        def body(c):
            it = c[1]
            snap = (it >= 11) & (it % 4 == 3)
            new = lax.cond(snap, snap_pass, bisect_pass, *c[2:])
            left = jnp.sum(1.0 - new[4])
            go = jnp.where((left > 0) & (it < 400), 1, 0)
            return (go, it + 1) + tuple(new)

        left0 = jnp.sum(1.0 - done0)
        state = (jnp.where(left0 > 0, 1, 0), jnp.int32(0), lo0, hi0, zeros, thr0, done0, zeros,
                 zeros, zeros)
        _, _, _, _, chi, thr, _, tie, _, _ = lax.while_loop(cond, body, state)

        big = jnp.full((qb_rows, 1), 2 ** 30, jnp.int32)

        def tie_break():
            need = topk - chi
            thr_b = jnp.broadcast_to(thr, (qb_rows, LANES))
            lane1 = lax.broadcasted_iota(jnp.int32, (1, LANES), 1)

            def tb_body(_, c):
                lo_j, hi_j = c
                mid_j = lo_j + (hi_j - lo_j) // 2
                mid_jb = jnp.broadcast_to(mid_j, (qb_rows, LANES))
                cnt = count_where(lambda t, piece, cidx: (piece == thr_b)
                                  & (t * ch + cidx * LANES + lane1 <= mid_jb))
                ok = cnt >= need
                return jnp.where(ok, lo_j, mid_j), jnp.where(ok, mid_j, hi_j)

            lo_j = jnp.full((qb_rows, 1), -1, jnp.int32)
            hi_j = jnp.zeros((qb_rows, 1), jnp.int32) + (n_ch * ch - 1)
            n_steps = int(np.ceil(np.log2(score_ref.shape[0] * ch))) + 1
            _, hi_j = lax.fori_loop(0, n_steps, tb_body, (lo_j, hi_j))
            return jnp.where(tie > 0, hi_j, big)

        jcut = lax.cond(jnp.sum(tie) > 0, tie_break, lambda: big)
        thr_ref[...] = jnp.broadcast_to(thr, (qb_rows, LANES))
        jcut_ref[...] = jnp.broadcast_to(jcut, (qb_rows, LANES))
        m_ref[...] = jnp.full(m_ref.shape, NEG_BIG, F32)
        l_ref[...] = jnp.zeros(l_ref.shape, F32)
        acc_ref[...] = jnp.zeros(acc_ref.shape, F32)

    thr_b = thr_ref[...]
    jcut_b = jcut_ref[...]
    lane1 = lax.broadcasted_iota(jnp.int32, (1, LANES), 1)
    bias = []
    for cc in range(n_ch_per_kt):
        blk = score_ref[kt * n_ch_per_kt + cc]
        for c in range(ch // LANES):
            piece = blk[:, c * LANES:(c + 1) * LANES]
            kpos = kt * kt_rows + cc * ch + c * LANES + lane1
            sel = (piece >= thr_b) & ((piece > thr_b) | (kpos <= jcut_b))
            bias.append(jnp.where(sel, 0.0, NEG_BIG))
    n_piece = len(bias)

    qa = qa_ref[0]
    for g in range(N_PAIRS):
        k_g = k_ref[0, :, g * LANES:(g + 1) * LANES]
        v_g = v_ref[0, :, g * LANES:(g + 1) * LANES]
        for half, q_h in enumerate(_head_halves(qa[:, g * LANES:(g + 1) * LANES])):
            h = 2 * g + half
            logits = _nt_dot(q_h, k_g)
            pieces = [logits[:, c * LANES:(c + 1) * LANES] + bias[c] for c in range(n_piece)]
            mx = pieces[0]
            for c in range(1, n_piece):
                mx = jnp.maximum(mx, pieces[c])
            m_old = m_ref[h]
            m_new = jnp.maximum(m_old, jnp.max(mx, axis=1, keepdims=True))
            alpha = jnp.exp(m_old - m_new)
            ps = [jnp.exp(pc - m_new) for pc in pieces]
            psum = ps[0]
            for c in range(1, n_piece):
                psum = psum + ps[c]
            p16 = jnp.concatenate([pc.astype(BF16) for pc in ps], axis=1)
            pv = jnp.dot(p16, v_g, preferred_element_type=F32)
            m_ref[h] = m_new
            l_ref[h] = alpha * l_ref[h] + psum
            acc_ref[h] = alpha * acc_ref[h] + pv

    @pl.when(last_of[i] == 1)
    def _finish():
        lane = lax.broadcasted_iota(jnp.int32, (qb_rows, LANES), 1)
        for g in range(N_PAIRS):
            outs = []
            for half in range(2):
                h = 2 * g + half
                denom = jnp.sum(l_ref[h], axis=1, keepdims=True)
                outs.append(acc_ref[h] / denom)
            o_ref[0, :, g * LANES:(g + 1) * LANES] = jnp.where(
                lane < HEAD_DIM, outs[0], outs[1]).astype(o_ref.dtype)


def _dsa(qa, qi, wi, ki16, k16, v16, *, pos0, l_valid, qb_rows, kt_rows, ch):
    bsz, tq, _ = qa.shape
    l_pad = k16.shape[1]
    assert tq % qb_rows == 0 and l_pad % kt_rows == 0 and kt_rows % ch == 0
    topk = min(TOPK_MAX, l_valid // 4)
    qb_l, kt_l, first_l, last_l = [], [], [], []
    for qb in range(tq // qb_rows):
        max_limit = min(((pos0 + qb * qb_rows + qb_rows - 1) // CHUNK + 1) * CHUNK, l_valid)
        n_kt = -(-max_limit // kt_rows)
        for kt in range(n_kt):
            qb_l.append(qb)
            kt_l.append(kt)
            first_l.append(int(kt == 0))
            last_l.append(int(kt == n_kt - 1))
    tables = [jnp.asarray(np.asarray(a, np.int32)) for a in (qb_l, kt_l, first_l, last_l)]
    q_map = lambda b, i, qb_of, kt_of, f, l: (b, qb_of[i], 0)
    kv_map = lambda b, i, qb_of, kt_of, f, l: (b, kt_of[i], 0)
    all_map = lambda b, i, qb_of, kt_of, f, l: (b, 0, 0)
    grid_spec = pltpu.PrefetchScalarGridSpec(
        num_scalar_prefetch=4,
        grid=(bsz, len(qb_l)),
        in_specs=[pl.BlockSpec((1, qb_rows, D_HEADS), q_map),
                  pl.BlockSpec((1, qb_rows, D_HEADS), q_map),
                  pl.BlockSpec((1, qb_rows, LANES), q_map),
                  pl.BlockSpec((1, l_pad, LANES), all_map),
                  pl.BlockSpec((1, kt_rows, D_HEADS), kv_map),
                  pl.BlockSpec((1, kt_rows, D_HEADS), kv_map)],
        out_specs=pl.BlockSpec((1, qb_rows, D_HEADS), q_map),
        scratch_shapes=[pltpu.VMEM((l_pad // ch, qb_rows, ch), F32),
                        pltpu.VMEM((qb_rows, LANES), F32),
                        pltpu.VMEM((qb_rows, LANES), jnp.int32),
                        pltpu.VMEM((N_HEADS, qb_rows, LANES), F32),
                        pltpu.VMEM((N_HEADS, qb_rows, LANES), F32),
                        pltpu.VMEM((N_HEADS, qb_rows, LANES), F32)])
    return pl.pallas_call(
        functools.partial(_dsa_kernel, qb_rows=qb_rows, kt_rows=kt_rows, ch=ch, pos0=pos0,
                          l_valid=l_valid, topk=topk),
        grid_spec=grid_spec,
        out_shape=jax.ShapeDtypeStruct((bsz, tq, D_HEADS), BF16),
        compiler_params=pltpu.CompilerParams(dimension_semantics=("arbitrary", "arbitrary"),
                                             vmem_limit_bytes=VMEM_LIMIT),
        name="dsa",
    )(*tables, qa, qi, wi, ki16, k16, v16)


def _sb_kernel(q_ref, k_ref, v_ref, o_ref, run_ref, acc_ref, *, qb_rows, pos0):
    qb = pl.program_id(1)
    qpos = pos0 + qb * qb_rows + lax.broadcasted_iota(jnp.int32, (qb_rows, 1), 0)
    kt_top = (pos0 + qb * qb_rows + qb_rows - 1) // LANES
    lane1 = lax.broadcasted_iota(jnp.int32, (1, LANES), 1)
    jj = lax.broadcasted_iota(jnp.int32, (LANES, 2 * LANES), 0)
    ss = lax.broadcasted_iota(jnp.int32, (LANES, 2 * LANES), 1)
    sum_rhs = jnp.where((jj > ss) | (ss >= LANES), 1.0, 0.0).astype(BF16)
    lane = lax.broadcasted_iota(jnp.int32, (qb_rows, LANES), 1)

    q = q_ref[0]
    q_heads = []
    for g in range(N_PAIRS):
        q_heads.extend(_head_halves(q[:, g * LANES:(g + 1) * LANES]))
    run_ref[...] = jnp.zeros(run_ref.shape, F32)
    acc_ref[...] = jnp.zeros(acc_ref.shape, F32)

    def cond(c):
        return c[0] > 0

    def body(c):
        _, kt = c
        rows = pl.ds(pl.multiple_of(kt * LANES, LANES), LANES)
        strict = (kt * LANES + lane1) < qpos
        alive = jnp.full((qb_rows, LANES), -jnp.inf, F32)
        for h in range(N_HEADS):
            g = h // 2
            k_t = k_ref[0, rows, g * LANES:(g + 1) * LANES]
            v_t = v_ref[0, rows, g * LANES:(g + 1) * LANES]
            z = _nt_dot(q_heads[h], k_t)
            sp = jnp.maximum(z, 0.0) + jnp.log1p(jnp.exp(-jnp.abs(z)))
            ls = jnp.where(strict, -sp, 0.0)
            ls_hi = ls.astype(BF16)
            ls_lo = (ls - ls_hi.astype(F32)).astype(BF16)
            sums = (jnp.dot(ls_hi, sum_rhs, preferred_element_type=F32)
                    + jnp.dot(ls_lo, sum_rhs, preferred_element_type=F32))
            run = run_ref[h]
            log_rest = run + sums[:, :LANES]
            wgt = jnp.where(strict, jnp.exp(z - sp + log_rest), 0.0)
            acc_ref[h] += jnp.dot(wgt.astype(BF16), v_t, preferred_element_type=F32)
            run = run + sums[:, LANES:]
            run_ref[h] = run
            alive = jnp.maximum(alive, run)
        go = jnp.where((kt > 0) & (jnp.max(alive) > SB_DEAD), 1, 0)
        return go, kt - 1

    lax.while_loop(cond, body, (jnp.int32(1), kt_top))
    for g in range(N_PAIRS):
        o_ref[0, :, g * LANES:(g + 1) * LANES] = jnp.where(
            lane < HEAD_DIM, acc_ref[2 * g], acc_ref[2 * g + 1]).astype(o_ref.dtype)


def _sb(qb16, k16, v16, *, pos0, qb_rows):
    bsz, tq, _ = qb16.shape
    l_pad = k16.shape[1]
    assert tq % qb_rows == 0 and l_pad % LANES == 0
    q_map = lambda b, i: (b, i, 0)
    all_map = lambda b, i: (b, 0, 0)
    resident = dict(pipeline_mode=pl.Buffered(1)) if bsz == 1 else {}
    return pl.pallas_call(
        functools.partial(_sb_kernel, qb_rows=qb_rows, pos0=pos0),
        grid=(bsz, tq // qb_rows),
        in_specs=[pl.BlockSpec((1, qb_rows, D_HEADS), q_map),
                  pl.BlockSpec((1, l_pad, D_HEADS), all_map, **resident),
                  pl.BlockSpec((1, l_pad, D_HEADS), all_map, **resident)],
        out_specs=pl.BlockSpec((1, qb_rows, D_HEADS), q_map),
        out_shape=jax.ShapeDtypeStruct((bsz, tq, D_HEADS), BF16),
        scratch_shapes=[pltpu.VMEM((N_HEADS, qb_rows, LANES), F32),
                        pltpu.VMEM((N_HEADS, qb_rows, LANES), F32)],
        compiler_params=pltpu.CompilerParams(dimension_semantics=("arbitrary", "arbitrary"),
                                             vmem_limit_bytes=VMEM_LIMIT),
        name="sb",
    )(qb16, k16, v16)


def _post_kernel(oa_ref, ob_ref, sga_ref, sgb_ref, x_ref, wa_ref, wb_ref, wo_ref, g_ref,
                 wr_hi_ref, wr_lo_ref, br_ref, h_o, hn_o, comb_o, *, tm):
    ya = jnp.dot(oa_ref[...], wa_ref[...], preferred_element_type=F32)
    yb = jnp.dot(ob_ref[...], wb_ref[...], preferred_element_type=F32)
    mix = sga_ref[...] * ya + sgb_ref[...] * yb
    h = x_ref[...] + jnp.dot(mix.astype(BF16), wo_ref[...], preferred_element_type=F32)
    h_o[...] = h
    r = lax.rsqrt(jnp.mean(h * h, axis=-1, keepdims=True) + RMS_EPS)
    hn = (h * r) * g_ref[...]
    hn_hi = hn.astype(BF16)
    hn_o[...] = hn_hi
    hn_lo = (hn - hn_hi.astype(F32)).astype(BF16)
    logits = (jnp.dot(hn_hi, wr_hi_ref[...], preferred_element_type=F32)
              + jnp.dot(hn_lo, wr_hi_ref[...], preferred_element_type=F32)
              + jnp.dot(hn_hi, wr_lo_ref[...], preferred_element_type=F32)) + br_ref[...]

    lane = lax.broadcasted_iota(jnp.int32, (tm, LANES), 1)
    is_g = (lane >= N_EXPERTS) & (lane < N_EXPERTS + N_GROUPS)
    gl = jnp.where(is_g, logits, -jnp.inf)
    gmax = jnp.max(gl, axis=1, keepdims=True)
    g_lane = jnp.min(jnp.where(gl == gmax, lane, 2 ** 30), axis=1, keepdims=True)
    g_w = 1.0 / jnp.sum(jnp.exp(gl - gmax), axis=1, keepdims=True)
    in_grp = (lane < N_EXPERTS) & ((lane // EXPERTS_PER_GROUP) == (g_lane - N_EXPERTS))
    e1 = jnp.where(in_grp, logits, -jnp.inf)
    v1 = jnp.max(e1, axis=1, keepdims=True)
    i1 = jnp.min(jnp.where(e1 == v1, lane, 2 ** 30), axis=1, keepdims=True)
    e2 = jnp.where(lane == i1, -jnp.inf, e1)
    v2 = jnp.max(e2, axis=1, keepdims=True)
    i2 = jnp.min(jnp.where(e2 == v2, lane, 2 ** 30), axis=1, keepdims=True)
    t2 = jnp.exp(v2 - v1)
    den = 1.0 + t2
    comb_o[...] = jnp.where(lane == i1, (1.0 / den) * g_w,
                            jnp.where(lane == i2, (t2 / den) * g_w, 0.0))


def _post(oa, ob, sga, sgb, x2d, wa16, wb16, wo16, g_ffn, wr_hi, wr_lo, br):
    n = x2d.shape[0]
    tm = min(256, n)
    assert n % tm == 0
    row = lambda i: (i, 0)
    fix = lambda i: (0, 0)
    return pl.pallas_call(
        functools.partial(_post_kernel, tm=tm),
        grid=(n // tm,),
        in_specs=[pl.BlockSpec((tm, D_HEADS), row), pl.BlockSpec((tm, D_HEADS), row),
                  pl.BlockSpec((tm, D_MODEL), row), pl.BlockSpec((tm, D_MODEL), row),
                  pl.BlockSpec((tm, D_MODEL), row),
                  pl.BlockSpec((D_HEADS, D_MODEL), fix), pl.BlockSpec((D_HEADS, D_MODEL), fix),
                  pl.BlockSpec((D_MODEL, D_MODEL), fix), pl.BlockSpec((1, D_MODEL), fix),
                  pl.BlockSpec((D_MODEL, LANES), fix), pl.BlockSpec((D_MODEL, LANES), fix),
                  pl.BlockSpec((1, LANES), fix)],
        out_specs=[pl.BlockSpec((tm, D_MODEL), row), pl.BlockSpec((tm, D_MODEL), row),
                   pl.BlockSpec((tm, LANES), row)],
        out_shape=[jax.ShapeDtypeStruct((n, D_MODEL), F32),
                   jax.ShapeDtypeStruct((n, D_MODEL), BF16),
                   jax.ShapeDtypeStruct((n, LANES), F32)],
        compiler_params=pltpu.CompilerParams(dimension_semantics=("arbitrary",),
                                             vmem_limit_bytes=VMEM_LIMIT),
        name="post",
    )(oa, ob, sga, sgb, x2d, wa16, wb16, wo16, g_ffn, wr_hi, wr_lo, br)


def _moe_kernel(hn_ref, comb_ref, h_ref, wg_ref, wu_ref, wd_ref, gf_ref, y_o, acc_ref, *, tm):
    e = pl.program_id(1)

    @pl.when(e == 0)
    def _init():
        acc_ref[...] = jnp.zeros(acc_ref.shape, F32)

    hn = hn_ref[...]
    gate = jnp.dot(hn, wg_ref[0], preferred_element_type=F32)
    up = jnp.dot(hn, wu_ref[0], preferred_element_type=F32)
    lane = lax.broadcasted_iota(jnp.int32, (tm, LANES), 1)
    c_e = jnp.sum(jnp.where(lane == e, comb_ref[...], 0.0), axis=1, keepdims=True)
    hid = (gate * jax.nn.sigmoid(gate)) * up * c_e
    acc_ref[...] += jnp.dot(hid.astype(BF16), wd_ref[0], preferred_element_type=F32)

    @pl.when(e == N_EXPERTS - 1)
    def _finish():
        out = h_ref[...] + acc_ref[...]
        r = lax.rsqrt(jnp.mean(out * out, axis=-1, keepdims=True) + RMS_EPS)
        y_o[...] = (out * r) * gf_ref[...]


def _moe(hn16, comb, h, wg16, wu16, wd16, g_final):
    n = h.shape[0]
    tm = min(1024, n)
    assert n % tm == 0
    row = lambda i, e: (i, 0)
    exp = lambda i, e: (e, 0, 0)
    fix = lambda i, e: (0, 0)
    return pl.pallas_call(
        functools.partial(_moe_kernel, tm=tm),
        grid=(n // tm, N_EXPERTS),
        in_specs=[pl.BlockSpec((tm, D_MODEL), row), pl.BlockSpec((tm, LANES), row),
                  pl.BlockSpec((tm, D_MODEL), row),
                  pl.BlockSpec((1, D_MODEL, D_FF_EXPERT), exp),
                  pl.BlockSpec((1, D_MODEL, D_FF_EXPERT), exp),
                  pl.BlockSpec((1, D_FF_EXPERT, D_MODEL), exp),
                  pl.BlockSpec((1, D_MODEL), fix)],
        out_specs=pl.BlockSpec((tm, D_MODEL), row),
        out_shape=jax.ShapeDtypeStruct((n, D_MODEL), F32),
        scratch_shapes=[pltpu.VMEM((tm, D_MODEL), F32)],
        compiler_params=pltpu.CompilerParams(dimension_semantics=("arbitrary", "arbitrary"),
                                             vmem_limit_bytes=VMEM_LIMIT),
        name="moe",
    )(hn16, comb, h, wg16, wu16, wd16, g_final)


def _prep_weights(norm_mix_g, w_in, w_br_a, w_br_b, w_out, norm_ffn_g, w_rg, b_rg, w_re, b_re,
                  w_eg, w_eu, w_ed, norm_final_g):
    offs = np.cumsum(IN_SPLIT)[:-1].tolist()
    q_a, k_a, v_a, q_i, k_i, w_i, q_b, k_b, v_b, g_a, g_b = jnp.split(w_in, offs, axis=-1)
    w_i_pad = jnp.pad(w_i, ((0, 0), (0, LANES - N_IDX_HEADS)))
    w16 = jnp.concatenate([q_a, k_a, v_a, q_i, k_i, k_i, w_i_pad, q_b, k_b, v_b, g_a, g_b],
                          axis=1).astype(BF16)
    half = ROT_HALF
    inv_freq = ROPE_THETA ** (-jnp.arange(half, dtype=F32) / half)
    d = np.arange(LANES) % HEAD_DIM
    invf = jnp.where(jnp.asarray(d < ROT_DIM), inv_freq[jnp.asarray(d % half)], 0.0)[None, :]
    pad = LANES - N_EXPERTS - N_GROUPS
    w_r = jnp.pad(jnp.concatenate([w_re, w_rg], axis=1), ((0, 0), (0, pad)))
    wr_hi = w_r.astype(BF16)
    wr_lo = (w_r - wr_hi.astype(F32)).astype(BF16)
    b_r = jnp.pad(jnp.concatenate([b_re, b_rg]), (0, pad))[None, :]
    return dict(g_mix=norm_mix_g[None, :], w16=w16, invf=invf,
                wa16=w_br_a.astype(BF16), wb16=w_br_b.astype(BF16), wo16=w_out.astype(BF16),
                g_ffn=norm_ffn_g[None, :], wr_hi=wr_hi, wr_lo=wr_lo, b_r=b_r,
                wg16=w_eg.astype(BF16), wu16=w_eu.astype(BF16), wd16=w_ed.astype(BF16),
                g_final=norm_final_g[None, :])


def _layer(x, pos0, past, wts, *, dsa_qb, dsa_kt, dsa_ch, sb_qb):
    bsz, t, _ = x.shape
    n = bsz * t
    x2d = x.reshape(n, D_MODEL)
    pr = _proj(x2d, wts["g_mix"], wts["w16"], wts["invf"], tq=t, pos0=pos0)
    shp = lambda a: a.reshape(bsz, t, a.shape[-1])

    if past is None:
        l_valid = t
        ka16, va16, ki16, kb16, vb16 = (shp(pr[k]) for k in ("ka16", "va16", "ki16", "kb16", "vb16"))
    else:
        p_len = past[0].shape[1]
        l_valid = p_len + t
        l_pad = -(-l_valid // dsa_kt) * dsa_kt

        def join(cache2d, new16):
            zeros = jnp.zeros((bsz, l_pad - l_valid, new16.shape[-1]), BF16)
            return jnp.concatenate([cache2d.astype(BF16), shp(new16), zeros], axis=1)

        flat = lambda c: c.reshape(bsz, p_len, -1)
        ka16 = join(flat(past[0]), pr["ka16"])
        va16 = join(flat(past[1]), pr["va16"])
        ki_c = flat(past[2])
        ki16 = join(jnp.concatenate([ki_c, ki_c], axis=-1), pr["ki16"])
        kb16 = join(flat(past[3]), pr["kb16"])
        vb16 = join(flat(past[4]), pr["vb16"])

    oa = _dsa(shp(pr["qa"]), shp(pr["qi"]), shp(pr["wi"]), ki16, ka16, va16, pos0=pos0,
              l_valid=l_valid, qb_rows=dsa_qb, kt_rows=dsa_kt, ch=dsa_ch)
    ob = _sb(shp(pr["qb"]), kb16, vb16, pos0=pos0, qb_rows=sb_qb)
    h, hn16, comb = _post(oa.reshape(n, D_HEADS), ob.reshape(n, D_HEADS), pr["sga"], pr["sgb"],
                          x2d, wts["wa16"], wts["wb16"], wts["wo16"], wts["g_ffn"],
                          wts["wr_hi"], wts["wr_lo"], wts["b_r"])
    y = _moe(hn16, comb, h, wts["wg16"], wts["wu16"], wts["wd16"], wts["g_final"])
    heads = lambda a: a.reshape(1, bsz, t, N_HEADS, HEAD_DIM)
    rows = (heads(pr["ka"]), heads(pr["va"]),
            pr["ki"][:, :IDX_DIM].reshape(1, bsz, t, IDX_DIM),
            heads(pr["kb"]), heads(pr["vb"]))
    return y.reshape(bsz, t, D_MODEL), rows


def kernel(x_prompt, x_sample, cache_a_k, cache_a_v, cache_idx_k, cache_b_k, cache_b_v,
           norm_mix_g, w_in, w_br_a, w_br_b, w_out, norm_ffn_g,
           w_router_group, b_router_group, w_router_expert, b_router_expert,
           w_exp_gate, w_exp_up, w_exp_down, norm_final_g):
    assert w_in.shape[0] == 1, "single-layer model"
    wts = _prep_weights(norm_mix_g[0], w_in[0], w_br_a[0], w_br_b[0], w_out[0], norm_ffn_g[0],
                        w_router_group[0], b_router_group[0], w_router_expert[0],
                        b_router_expert[0], w_exp_gate[0], w_exp_up[0], w_exp_down[0],
                        norm_final_g)
    y_p, rows_p = _layer(x_prompt, 0, None, wts, dsa_qb=256, dsa_kt=1024, dsa_ch=512, sb_qb=128)
    past = (cache_a_k[0], cache_a_v[0], cache_idx_k[0], cache_b_k[0], cache_b_v[0])
    p_len = cache_a_k.shape[2]
    t_s = x_sample.shape[1]
    l_s = -(-(p_len + t_s) // 512) * 512
    y_s, rows_s = _layer(x_sample, p_len, past, wts, dsa_qb=t_s, dsa_kt=l_s, dsa_ch=512, sb_qb=t_s)
    return (y_p, y_s) + rows_p + rows_s
```

```python
import functools

import numpy as np
import jax
import jax.numpy as jnp
from jax import lax
from jax.experimental import pallas as pl
from jax.experimental.pallas import tpu as pltpu

D_MODEL = 1024
HEAD_DIM = 64
N_HEADS = 8
D_HEADS = N_HEADS * HEAD_DIM
N_IDX_HEADS = 8
IDX_DIM = 64
IDX_SCALE = (N_IDX_HEADS * IDX_DIM) ** -0.5
ATT_SCALE = HEAD_DIM ** -0.5
LOG2_E = 1.4426950408889634
CHUNK = 64
TOPK_MAX = 256
ROPE_THETA = 500000.0
ROT_DIM = HEAD_DIM // 4
ROT_HALF = ROT_DIM // 2
N_GROUPS = 4
EXPERTS_PER_GROUP = 8
N_EXPERTS = N_GROUPS * EXPERTS_PER_GROUP
D_FF_EXPERT = D_MODEL // 4
RMS_EPS = 1e-6
IN_SPLIT = (D_HEADS, D_HEADS, D_HEADS, N_IDX_HEADS * IDX_DIM, IDX_DIM, N_IDX_HEADS,
            D_HEADS, D_HEADS, D_HEADS, D_MODEL, D_MODEL)

LANES = 128
SUBLANES = 8
N_PAIRS = N_HEADS // 2
NEG_BIG = -1e30
SB_DEAD = -110.0
VMEM_LIMIT = 56 * 1024 * 1024

C_QA, C_KA, C_VA, C_QI, C_KI, C_WI, C_QB, C_KB, C_VB, C_GA, C_GB, C_END = (
    0, 512, 1024, 1536, 2048, 2176, 2304, 2816, 3328, 3840, 4864, 5888)

F32 = jnp.float32
BF16 = jnp.bfloat16


def _nt_dot(a, b):
    return lax.dot_general(a, b, (((1,), (1,)), ((), ())), preferred_element_type=F32)


def _reduce0(x, op, ways=8):
    n = x.shape[0]
    if n % ways != 0 or n <= ways:
        ways = 1
    part = x.reshape(n // ways, ways, *x.shape[1:])
    acc = part[0]
    for j in range(1, n // ways):
        acc = op(acc, part[j])
    out = acc[0]
    for j in range(1, ways):
        out = op(out, acc[j])
    return out


def _head_halves(x):
    lane = lax.broadcasted_iota(jnp.int32, x.shape, 1)
    zero = jnp.zeros_like(x)
    return jnp.where(lane < HEAD_DIM, x, zero), jnp.where(lane >= HEAD_DIM, x, zero)


def _proj_kernel(x_ref, g_ref, w_ref, invf_ref,
                 qa_o, qi_o, qb_o, ka_o, va_o, kb_o, vb_o, ki_o, wi_o,
                 ka16_o, va16_o, kb16_o, vb16_o, ki16_o, sga_o, sgb_o,
                 *, tm, tq, pos0, qa_scale):
    x = x_ref[...]
    r = lax.rsqrt(jnp.mean(x * x, axis=-1, keepdims=True) + RMS_EPS)
    xn = (x * r) * g_ref[...]
    p = jnp.dot(xn.astype(BF16), w_ref[...], preferred_element_type=F32)

    row = pl.program_id(0) * tm + lax.broadcasted_iota(jnp.int32, (tm, 1), 0)
    pos = (row % tq + pos0).astype(F32)
    ang = pos * invf_ref[...]
    c = jnp.cos(ang)
    s = jnp.sin(ang)
    d = lax.broadcasted_iota(jnp.int32, (1, LANES), 1) % HEAD_DIM
    s_lo = jnp.where(d < ROT_HALF, -s, 0.0)
    s_hi = jnp.where((d >= ROT_HALF) & (d < ROT_DIM), s, 0.0)

    def rope(v):
        return (v * c + pltpu.roll(v, LANES - ROT_HALF, 1) * s_lo
                + pltpu.roll(v, ROT_HALF, 1) * s_hi)

    for j in range(D_HEADS // LANES):
        sl = slice(j * LANES, (j + 1) * LANES)
        qa = rope(p[:, C_QA + j * LANES:C_QA + (j + 1) * LANES])
        qa_o[:, sl] = (qa * qa_scale).astype(BF16)
        ka = rope(p[:, C_KA + j * LANES:C_KA + (j + 1) * LANES])
        ka_o[:, sl] = ka
        ka16_o[:, sl] = ka.astype(BF16)
        qi = rope(p[:, C_QI + j * LANES:C_QI + (j + 1) * LANES])
        qi_o[:, sl] = qi.astype(BF16)
    ki = rope(p[:, C_KI:C_KI + LANES])
    ki_o[...] = ki
    ki16_o[...] = ki.astype(BF16)
    wi_o[...] = p[:, C_WI:C_WI + LANES]
    va = p[:, C_VA:C_VA + D_HEADS]
    va_o[...] = va
    va16_o[...] = va.astype(BF16)
    qb_o[...] = (p[:, C_QB:C_QB + D_HEADS] * ATT_SCALE).astype(BF16)
    kb = p[:, C_KB:C_KB + D_HEADS]
    kb_o[...] = kb
    kb16_o[...] = kb.astype(BF16)
    vb = p[:, C_VB:C_VB + D_HEADS]
    vb_o[...] = vb
    vb16_o[...] = vb.astype(BF16)
    sga_o[...] = jax.nn.sigmoid(p[:, C_GA:C_GA + D_MODEL])
    sgb_o[...] = jax.nn.sigmoid(p[:, C_GB:C_GB + D_MODEL])


def _proj(x2d, g, w16, invf, *, tq, pos0, qa_scale):
    n = x2d.shape[0]
    tm = min(256, n)
    assert n % tm == 0
    row = lambda i: (i, 0)
    fix = lambda i: (0, 0)
    widths = dict(qa=D_HEADS, qi=D_HEADS, qb=D_HEADS, ka=D_HEADS, va=D_HEADS, kb=D_HEADS,
                  vb=D_HEADS, ki=LANES, wi=LANES, ka16=D_HEADS, va16=D_HEADS, kb16=D_HEADS,
                  vb16=D_HEADS, ki16=LANES, sga=D_MODEL, sgb=D_MODEL)
    dtypes = dict(qa=BF16, qi=BF16, qb=BF16, ka16=BF16, va16=BF16, kb16=BF16, vb16=BF16, ki16=BF16)
    names = list(widths)
    out_shape = [jax.ShapeDtypeStruct((n, widths[k]), dtypes.get(k, F32)) for k in names]
    out_specs = [pl.BlockSpec((tm, widths[k]), row) for k in names]
    outs = pl.pallas_call(
        functools.partial(_proj_kernel, tm=tm, tq=tq, pos0=pos0, qa_scale=qa_scale),
        grid=(n // tm,),
        in_specs=[pl.BlockSpec((tm, D_MODEL), row),
                  pl.BlockSpec((1, D_MODEL), fix),
                  pl.BlockSpec((D_MODEL, C_END), fix),
                  pl.BlockSpec((1, LANES), fix)],
        out_specs=out_specs,
        out_shape=out_shape,
        compiler_params=pltpu.CompilerParams(dimension_semantics=("arbitrary",),
                                             vmem_limit_bytes=VMEM_LIMIT),
        name="proj",
    )(x2d, g, w16, invf)
    return dict(zip(names, outs))


def _dsa_kernel(qb_of, kt_of, first_of, last_of,
                qa_ref, qi_ref, wi_ref, ki_ref, k_ref, v_ref, o_ref,
                score_ref, thr_ref, jcut_ref, m_ref, l_ref, acc_ref,
                *, qb_rows, kt_rows, ch, pos0, l_valid, topk):
    i = pl.program_id(1)
    qb = qb_of[i]
    kt = kt_of[i]
    n_ch_per_kt = kt_rows // ch

    row = qb * qb_rows + lax.broadcasted_iota(jnp.int32, (qb_rows, 1), 0)
    limit = jnp.minimum(((pos0 + row) // CHUNK + 1) * CHUNK, l_valid)
    max_limit = jnp.minimum(((pos0 + qb * qb_rows + qb_rows - 1) // CHUNK + 1) * CHUNK, l_valid)
    n_kt = (max_limit + kt_rows - 1) // kt_rows
    n_ch = n_kt * n_ch_per_kt

    def count_where(pred_fn):
        def body(t, acc):
            blk = score_ref[t]
            for c in range(ch // LANES):
                acc = acc + jnp.where(pred_fn(t, blk[:, c * LANES:(c + 1) * LANES], c), 1.0, 0.0)
            return acc
        acc = lax.fori_loop(0, n_ch, body, jnp.zeros((qb_rows, LANES), F32))
        return jnp.sum(acc, axis=1, keepdims=True)

    @pl.when(first_of[i] == 1)
    def _select():
        qi = qi_ref[0]
        q_halves = []
        for g in range(N_PAIRS):
            q_halves.extend(_head_halves(qi[:, g * LANES:(g + 1) * LANES]))
        w = wi_ref[0] * IDX_SCALE
        w_cols = [w[:, h:h + 1] for h in range(N_IDX_HEADS)]
        lane = lax.broadcasted_iota(jnp.int32, (1, ch), 1)

        def score_body(t, carry):
            kk = ki_ref[0, pl.ds(pl.multiple_of(t * ch, ch), ch), :]
            s = jnp.zeros((qb_rows, ch), F32)
            for h in range(N_IDX_HEADS):
                s = s + w_cols[h] * jnp.maximum(_nt_dot(q_halves[h], kk), 0.0)
            kpos = t * ch + lane
            score_ref[t] = jnp.where(kpos < limit, s, -jnp.inf)
            return carry
        lax.fori_loop(0, n_ch, score_body, 0)

        def minmax_body(t, carry):
            mn, mx = carry
            blk = score_ref[t]
            for c in range(ch // LANES):
                piece = blk[:, c * LANES:(c + 1) * LANES]
                mx = jnp.maximum(mx, piece)
                mn = jnp.minimum(mn, jnp.where(piece == -jnp.inf, jnp.inf, piece))
            return mn, mx
        mn, mx = lax.fori_loop(0, n_ch, minmax_body,
                               (jnp.full((qb_rows, LANES), jnp.inf, F32),
                                jnp.full((qb_rows, LANES), -jnp.inf, F32)))
        lo0 = jnp.min(mn, axis=1, keepdims=True)
        smax = jnp.max(mx, axis=1, keepdims=True)
        hi0 = smax + (jnp.abs(smax) * 1e-6 + 1e-30)
        all_in = limit <= topk
        done0 = jnp.where(all_in, 1.0, 0.0)
        thr0 = jnp.where(all_in, jnp.float32(-3e38), lo0)
        zeros = jnp.zeros((qb_rows, 1), F32)

        def cond(c):
            return c[0] > 0

        def bisect_pass(lo, hi, chi, thr, done, tie, cand, pend):
            half = lo + 0.5 * (hi - lo)
            has_cand = pend > 0.5
            mid = jnp.where(has_cand, cand, half)
            stuck = jnp.logical_not(has_cand) & ((half <= lo) | (half >= hi))
            mid_b = jnp.broadcast_to(mid, (qb_rows, LANES))
            cnt = count_where(lambda t, piece, cidx: piece >= mid_b)
            active = done < 0.5
            moving = active & jnp.logical_not(stuck)
            found = moving & ((cnt == topk) | (has_cand & (cnt > topk)))
            now_tie = (active & stuck) | (moving & has_cand & (cnt > topk))
            up = moving & jnp.logical_not(has_cand) & (cnt > topk)
            down = moving & (cnt < topk)
            thr = jnp.where(found, mid, jnp.where(active & stuck, lo, thr))
            tie = jnp.where(now_tie, 1.0, tie)
            done = jnp.where(found | now_tie, 1.0, done)
            lo = jnp.where(up, mid, lo)
            chi = jnp.where(down, cnt, chi)
            hi = jnp.where(down, mid, hi)
            return lo, hi, chi, thr, done, tie, cand, jnp.zeros_like(pend)

        def snap_pass(lo, hi, chi, thr, done, tie, cand, pend):
            hi_b = jnp.broadcast_to(hi, (qb_rows, LANES))

            def body(t, acc):
                blk = score_ref[t]
                for c in range(ch // LANES):
                    piece = blk[:, c * LANES:(c + 1) * LANES]
                    acc = jnp.maximum(acc, jnp.where(piece < hi_b, piece, -jnp.inf))
                return acc
            acc = lax.fori_loop(0, n_ch, body, jnp.full((qb_rows, LANES), -jnp.inf, F32))
            cand = jnp.max(acc, axis=1, keepdims=True)
            return lo, hi, chi, thr, done, tie, cand, 1.0 - done

        def body(c):
            it = c[1]
            snap = (it >= 11) & (it % 4 == 3)
            new = lax.cond(snap, snap_pass, bisect_pass, *c[2:])
            left = jnp.sum(1.0 - new[4])
            go = jnp.where((left > 0) & (it < 400), 1, 0)
            return (go, it + 1) + tuple(new)

        left0 = jnp.sum(1.0 - done0)
        state = (jnp.where(left0 > 0, 1, 0), jnp.int32(0), lo0, hi0, zeros, thr0, done0, zeros,
                 zeros, zeros)
        _, _, _, _, chi, thr, _, tie, _, _ = lax.while_loop(cond, body, state)

        big = jnp.full((qb_rows, 1), 2 ** 30, jnp.int32)

        def tie_break():
            need = topk - chi
            thr_b = jnp.broadcast_to(thr, (qb_rows, LANES))
            lane1 = lax.broadcasted_iota(jnp.int32, (1, LANES), 1)

            def tb_body(_, c):
                lo_j, hi_j = c
                mid_j = lo_j + (hi_j - lo_j) // 2
                mid_jb = jnp.broadcast_to(mid_j, (qb_rows, LANES))
                cnt = count_where(lambda t, piece, cidx: (piece == thr_b)
                                  & (t * ch + cidx * LANES + lane1 <= mid_jb))
                ok = cnt >= need
                return jnp.where(ok, lo_j, mid_j), jnp.where(ok, mid_j, hi_j)

            lo_j = jnp.full((qb_rows, 1), -1, jnp.int32)
            hi_j = jnp.zeros((qb_rows, 1), jnp.int32) + (n_ch * ch - 1)
            n_steps = int(np.ceil(np.log2(score_ref.shape[0] * ch))) + 1
            _, hi_j = lax.fori_loop(0, n_steps, tb_body, (lo_j, hi_j))
            return jnp.where(tie > 0, hi_j, big)

        jcut = lax.cond(jnp.sum(tie) > 0, tie_break, lambda: big)
        thr_ref[...] = jnp.broadcast_to(thr, (qb_rows, LANES))
        jcut_ref[...] = jnp.broadcast_to(jcut, (qb_rows, LANES))
        m_ref[...] = jnp.full(m_ref.shape, NEG_BIG, F32)
        l_ref[...] = jnp.zeros(l_ref.shape, F32)
        acc_ref[...] = jnp.zeros(acc_ref.shape, F32)

    thr_b = thr_ref[...]
    jcut_b = jcut_ref[...]
    lane1 = lax.broadcasted_iota(jnp.int32, (1, LANES), 1)
    bias = []
    for cc in range(n_ch_per_kt):
        blk = score_ref[kt * n_ch_per_kt + cc]
        for c in range(ch // LANES):
            piece = blk[:, c * LANES:(c + 1) * LANES]
            kpos = kt * kt_rows + cc * ch + c * LANES + lane1
            sel = (piece >= thr_b) & ((piece > thr_b) | (kpos <= jcut_b))
            bias.append(jnp.where(sel, 0.0, NEG_BIG))
    n_piece = len(bias)

    qa = qa_ref[0]
    for g in range(N_PAIRS):
        k_g = k_ref[0, :, g * LANES:(g + 1) * LANES]
        v_g = v_ref[0, :, g * LANES:(g + 1) * LANES]
        for half, q_h in enumerate(_head_halves(qa[:, g * LANES:(g + 1) * LANES])):
            h = 2 * g + half
            logits = _nt_dot(q_h, k_g)
            pieces = [logits[:, c * LANES:(c + 1) * LANES] + bias[c] for c in range(n_piece)]
            mx = pieces[0]
            for c in range(1, n_piece):
                mx = jnp.maximum(mx, pieces[c])
            m_old = m_ref[h]
            m_new = jnp.maximum(m_old, jnp.max(mx, axis=1, keepdims=True))
            alpha = jnp.exp(m_old - m_new)
            ps = [jnp.exp(pc - m_new) for pc in pieces]
            psum = ps[0]
            for c in range(1, n_piece):
                psum = psum + ps[c]
            p16 = jnp.concatenate([pc.astype(BF16) for pc in ps], axis=1)
            pv = jnp.dot(p16, v_g, preferred_element_type=F32)
            m_ref[h] = m_new
            l_ref[h] = alpha * l_ref[h] + psum
            acc_ref[h] = alpha * acc_ref[h] + pv

    @pl.when(last_of[i] == 1)
    def _finish():
        lane = lax.broadcasted_iota(jnp.int32, (qb_rows, LANES), 1)
        for g in range(N_PAIRS):
            outs = []
            for half in range(2):
                h = 2 * g + half
                denom = jnp.sum(l_ref[h], axis=1, keepdims=True)
                outs.append(acc_ref[h] / denom)
            o_ref[0, :, g * LANES:(g + 1) * LANES] = jnp.where(
                lane < HEAD_DIM, outs[0], outs[1]).astype(o_ref.dtype)


def _dsat_kernel(qb_of, kt_of, first_of, last_of,
                 qa_ref, qi_ref, wi_ref, ki_ref, k_ref, vt_ref, o_ref,
                 score_ref, m_ref, acc_ref,
                 *, qb_rows, kt_rows, pos0, l_valid, topk):
    i = pl.program_id(1)
    qb = qb_of[i]
    kt = kt_of[i]
    nq = qb_rows
    ns = kt_rows // SUBLANES

    qcol = qb * nq + lax.broadcasted_iota(jnp.int32, (SUBLANES, nq), 1)
    limit = jnp.minimum(((pos0 + qcol) // CHUNK + 1) * CHUNK, l_valid)
    max_limit = jnp.minimum(((pos0 + qb * nq + nq - 1) // CHUNK + 1) * CHUNK, l_valid)
    n_kt = (max_limit + kt_rows - 1) // kt_rows
    kidx = (lax.broadcasted_iota(jnp.int32, (ns, SUBLANES, nq), 0) * SUBLANES
            + lax.broadcasted_iota(jnp.int32, (ns, SUBLANES, nq), 1))

    def reduce_tiles(fn, op, init):
        def body(t, acc):
            return op(acc, fn(t, score_ref[t]))
        return lax.fori_loop(0, n_kt, body, init)

    def count_where(pred):
        acc = reduce_tiles(lambda t, blk: _reduce0(jnp.where(pred(t, blk), 1.0, 0.0), jnp.add),
                           jnp.add, jnp.zeros((SUBLANES, nq), F32))
        return jnp.broadcast_to(jnp.sum(acc, axis=0, keepdims=True), (SUBLANES, nq))

    @pl.when(first_of[i] == 1)
    def _select():
        qi = qi_ref[0]
        q_halves = []
        for g in range(N_PAIRS):
            q_halves.extend(_head_halves(qi[:, g * LANES:(g + 1) * LANES]))
        w_t = wi_ref[0].T * IDX_SCALE
        w_rows = [jnp.broadcast_to(w_t[h:h + 1, :], (SUBLANES, nq)) for h in range(N_IDX_HEADS)]

        def score_body(t, carry):
            kk = ki_ref[0, pl.ds(pl.multiple_of(t * kt_rows, kt_rows), kt_rows), :]
            s = jnp.zeros((ns, SUBLANES, nq), F32)
            for h in range(N_IDX_HEADS):
                rel = jnp.maximum(_nt_dot(kk, q_halves[h]), 0.0)
                s = s + w_rows[h] * rel.reshape(ns, SUBLANES, nq)
            score_ref[t] = jnp.where(kidx + t * kt_rows < limit, s, -jnp.inf)
            return carry
        lax.fori_loop(0, n_kt, score_body, 0)

        mn8 = reduce_tiles(
            lambda t, blk: _reduce0(jnp.where(blk == -jnp.inf, jnp.inf, blk), jnp.minimum),
            jnp.minimum, jnp.full((SUBLANES, nq), jnp.inf, F32))
        mx8 = reduce_tiles(lambda t, blk: _reduce0(blk, jnp.maximum),
                           jnp.maximum, jnp.full((SUBLANES, nq), -jnp.inf, F32))
        rep = lambda v: jnp.broadcast_to(v, (SUBLANES, nq))
        lo0 = rep(jnp.min(mn8, axis=0, keepdims=True))
        smax = rep(jnp.max(mx8, axis=0, keepdims=True))
        hi0 = smax + (jnp.abs(smax) * 1e-6 + 1e-30)
        all_in = limit <= topk
        done0 = jnp.where(all_in, 1.0, 0.0)
        thr0 = jnp.where(all_in, jnp.float32(-3e38), lo0)
        zeros = jnp.zeros((SUBLANES, nq), F32)

        def bisect_pass(lo, hi, chi, thr, done, tie, cand, pend):
            half = lo + 0.5 * (hi - lo)
            has_cand = pend > 0.5
            mid = jnp.where(has_cand, cand, half)
            stuck = jnp.logical_not(has_cand) & ((half <= lo) | (half >= hi))
            cnt = count_where(lambda t, blk: blk >= mid)
            active = done < 0.5
            moving = active & jnp.logical_not(stuck)
            found = moving & ((cnt == topk) | (has_cand & (cnt > topk)))
            now_tie = (active & stuck) | (moving & has_cand & (cnt > topk))
            up = moving & jnp.logical_not(has_cand) & (cnt > topk)
            down = moving & (cnt < topk)
            thr = jnp.where(found, mid, jnp.where(active & stuck, lo, thr))
            tie = jnp.where(now_tie, 1.0, tie)
            done = jnp.where(found | now_tie, 1.0, done)
            lo = jnp.where(up, mid, lo)
            chi = jnp.where(down, cnt, chi)
            hi = jnp.where(down, mid, hi)
            return lo, hi, chi, thr, done, tie, cand, jnp.zeros_like(pend)

        def snap_pass(lo, hi, chi, thr, done, tie, cand, pend):
            below = reduce_tiles(
                lambda t, blk: _reduce0(jnp.where(blk < hi, blk, -jnp.inf), jnp.maximum),
                jnp.maximum, jnp.full((SUBLANES, nq), -jnp.inf, F32))
            cand = rep(jnp.max(below, axis=0, keepdims=True))
            return lo, hi, chi, thr, done, tie, cand, 1.0 - done

        def cond(c):
            return c[0] > 0

        def body(c):
            it = c[1]
            snap = (it >= 11) & (it % 4 == 3)
            new = lax.cond(snap, snap_pass, bisect_pass, *c[2:])
            left = jnp.sum(1.0 - new[4])
            go = jnp.where((left > 0) & (it < 400), 1, 0)
            return (go, it + 1) + tuple(new)

        left0 = jnp.sum(1.0 - done0)
        state = (jnp.where(left0 > 0, 1, 0), jnp.int32(0), lo0, hi0, zeros, thr0, done0, zeros,
                 zeros, zeros)
        _, _, _, _, chi, thr, _, tie, _, _ = lax.while_loop(cond, body, state)

        big = jnp.full((SUBLANES, nq), 2 ** 30, jnp.int32)

        def tie_break():
            need = topk - chi

            def tb_body(_, c):
                lo_j, hi_j = c
                mid_j = lo_j + (hi_j - lo_j) // 2
                cnt = count_where(lambda t, blk: (blk == thr) & (kidx + t * kt_rows <= mid_j))
                ok = cnt >= need
                return jnp.where(ok, lo_j, mid_j), jnp.where(ok, mid_j, hi_j)

            lo_j = jnp.full((SUBLANES, nq), -1, jnp.int32)
            hi_j = jnp.zeros((SUBLANES, nq), jnp.int32) + (n_kt * kt_rows - 1)
            n_steps = int(np.ceil(np.log2(score_ref.shape[0] * kt_rows))) + 1
            _, hi_j = lax.fori_loop(0, n_steps, tb_body, (lo_j, hi_j))
            return jnp.where(tie > 0, hi_j, big)

        jcut = lax.cond(jnp.sum(tie) > 0, tie_break, lambda: big)

        def mask_body(t, carry):
            blk = score_ref[t]
            sel = (blk >= thr) & ((blk > thr) | (kidx + t * kt_rows <= jcut))
            score_ref[t] = jnp.where(sel, 0.0, NEG_BIG)
            return carry
        lax.fori_loop(0, n_kt, mask_body, 0)

        m_ref[...] = jnp.full(m_ref.shape, NEG_BIG, F32)
        acc_ref[...] = jnp.zeros(acc_ref.shape, F32)

    bias = score_ref[kt]
    qa = qa_ref[0]
    q_heads = []
    for g in range(N_PAIRS):
        q_heads.extend(_head_halves(qa[:, g * LANES:(g + 1) * LANES]))
    for h0 in range(0, N_HEADS, 4):
        heads = range(h0, h0 + 4)
        logits = {h: _nt_dot(k_ref[0, :, (h // 2) * LANES:(h // 2 + 1) * LANES], q_heads[h])
                  .reshape(ns, SUBLANES, nq) + bias for h in heads}
        m_old = {h: m_ref[h] for h in heads}
        m_new = {h: jnp.maximum(m_old[h], jnp.max(_reduce0(logits[h], jnp.maximum),
                                                  axis=0, keepdims=True)) for h in heads}
        alpha = {h: jnp.exp2(m_old[h] - m_new[h]) for h in heads}
        p = {h: jnp.exp2(logits[h] - m_new[h]) for h in heads}
        pv = {h: jnp.dot(vt_ref[0, h * LANES:(h + 1) * LANES, :],
                         p[h].reshape(kt_rows, nq).astype(BF16), preferred_element_type=F32)
              for h in heads}
        for h in heads:
            acc_ref[h] = alpha[h][0:1] * acc_ref[h] + pv[h]
            m_ref[h] = m_new[h]

    @pl.when(last_of[i] == 1)
    def _finish():
        for g in range(N_PAIRS):
            outs = []
            for h in (2 * g, 2 * g + 1):
                acc = acc_ref[h]
                outs.append(acc[:HEAD_DIM] / acc[HEAD_DIM:HEAD_DIM + 1])
            o_t = jnp.concatenate(outs, axis=0)
            o_ref[0, :, g * LANES:(g + 1) * LANES] = o_t.T.astype(o_ref.dtype)


def _dsat(qa, qi, wi, ki16, k16, vt16, *, pos0, l_valid, qb_rows, kt_rows):
    bsz, tq, _ = qa.shape
    l_pad = k16.shape[1]
    assert tq % qb_rows == 0 and l_pad % kt_rows == 0 and qb_rows % LANES == 0
    topk = min(TOPK_MAX, l_valid // 4)
    tables = _dsa_tables(tq, qb_rows, kt_rows, pos0, l_valid)
    q_map = lambda b, i, qb_of, kt_of, f, l: (b, qb_of[i], 0)
    k_map = lambda b, i, qb_of, kt_of, f, l: (b, kt_of[i], 0)
    vt_map = lambda b, i, qb_of, kt_of, f, l: (b, 0, kt_of[i])
    all_map = lambda b, i, qb_of, kt_of, f, l: (b, 0, 0)
    grid_spec = pltpu.PrefetchScalarGridSpec(
        num_scalar_prefetch=4,
        grid=(bsz, tables[0].shape[0]),
        in_specs=[pl.BlockSpec((1, qb_rows, D_HEADS), q_map),
                  pl.BlockSpec((1, qb_rows, D_HEADS), q_map),
                  pl.BlockSpec((1, qb_rows, LANES), q_map),
                  pl.BlockSpec((1, l_pad, LANES), all_map),
                  pl.BlockSpec((1, kt_rows, D_HEADS), k_map),
                  pl.BlockSpec((1, N_HEADS * LANES, kt_rows), vt_map)],
        out_specs=pl.BlockSpec((1, qb_rows, D_HEADS), q_map),
        scratch_shapes=[pltpu.VMEM((l_pad // kt_rows, kt_rows // SUBLANES, SUBLANES, qb_rows), F32),
                        pltpu.VMEM((N_HEADS, SUBLANES, qb_rows), F32),
                        pltpu.VMEM((N_HEADS, LANES, qb_rows), F32)])
    return pl.pallas_call(
        functools.partial(_dsat_kernel, qb_rows=qb_rows, kt_rows=kt_rows, pos0=pos0,
                          l_valid=l_valid, topk=topk),
        grid_spec=grid_spec,
        out_shape=jax.ShapeDtypeStruct((bsz, tq, D_HEADS), BF16),
        compiler_params=pltpu.CompilerParams(dimension_semantics=("arbitrary", "arbitrary"),
                                             vmem_limit_bytes=VMEM_LIMIT),
        name="dsat",
    )(*tables, qa, qi, wi, ki16, k16, vt16)


def _dsa_tables(tq, qb_rows, kt_rows, pos0, l_valid):
    qb_l, kt_l, first_l, last_l = [], [], [], []
    for qb in range(tq // qb_rows):
        max_limit = min(((pos0 + qb * qb_rows + qb_rows - 1) // CHUNK + 1) * CHUNK, l_valid)
        n_kt = -(-max_limit // kt_rows)
        for kt in range(n_kt):
            qb_l.append(qb)
            kt_l.append(kt)
            first_l.append(int(kt == 0))
            last_l.append(int(kt == n_kt - 1))
    return [jnp.asarray(np.asarray(a, np.int32)) for a in (qb_l, kt_l, first_l, last_l)]


def _dsa(qa, qi, wi, ki16, k16, v16, *, pos0, l_valid, qb_rows, kt_rows, ch):
    bsz, tq, _ = qa.shape
    l_pad = k16.shape[1]
    assert tq % qb_rows == 0 and l_pad % kt_rows == 0 and kt_rows % ch == 0
    topk = min(TOPK_MAX, l_valid // 4)
    qb_l, kt_l, first_l, last_l = [], [], [], []
    for qb in range(tq // qb_rows):
        max_limit = min(((pos0 + qb * qb_rows + qb_rows - 1) // CHUNK + 1) * CHUNK, l_valid)
        n_kt = -(-max_limit // kt_rows)
        for kt in range(n_kt):
            qb_l.append(qb)
            kt_l.append(kt)
            first_l.append(int(kt == 0))
            last_l.append(int(kt == n_kt - 1))
    tables = [jnp.asarray(np.asarray(a, np.int32)) for a in (qb_l, kt_l, first_l, last_l)]
    q_map = lambda b, i, qb_of, kt_of, f, l: (b, qb_of[i], 0)
    kv_map = lambda b, i, qb_of, kt_of, f, l: (b, kt_of[i], 0)
    all_map = lambda b, i, qb_of, kt_of, f, l: (b, 0, 0)
    grid_spec = pltpu.PrefetchScalarGridSpec(
        num_scalar_prefetch=4,
        grid=(bsz, len(qb_l)),
        in_specs=[pl.BlockSpec((1, qb_rows, D_HEADS), q_map),
                  pl.BlockSpec((1, qb_rows, D_HEADS), q_map),
                  pl.BlockSpec((1, qb_rows, LANES), q_map),
                  pl.BlockSpec((1, l_pad, LANES), all_map),
                  pl.BlockSpec((1, kt_rows, D_HEADS), kv_map),
                  pl.BlockSpec((1, kt_rows, D_HEADS), kv_map)],
        out_specs=pl.BlockSpec((1, qb_rows, D_HEADS), q_map),
        scratch_shapes=[pltpu.VMEM((l_pad // ch, qb_rows, ch), F32),
                        pltpu.VMEM((qb_rows, LANES), F32),
                        pltpu.VMEM((qb_rows, LANES), jnp.int32),
                        pltpu.VMEM((N_HEADS, qb_rows, LANES), F32),
                        pltpu.VMEM((N_HEADS, qb_rows, LANES), F32),
                        pltpu.VMEM((N_HEADS, qb_rows, LANES), F32)])
    return pl.pallas_call(
        functools.partial(_dsa_kernel, qb_rows=qb_rows, kt_rows=kt_rows, ch=ch, pos0=pos0,
                          l_valid=l_valid, topk=topk),
        grid_spec=grid_spec,
        out_shape=jax.ShapeDtypeStruct((bsz, tq, D_HEADS), BF16),
        compiler_params=pltpu.CompilerParams(dimension_semantics=("arbitrary", "arbitrary"),
                                             vmem_limit_bytes=VMEM_LIMIT),
        name="dsa",
    )(*tables, qa, qi, wi, ki16, k16, v16)


def _sb_kernel(q_ref, k_ref, v_ref, o_ref, run_ref, acc_ref, *, qb_rows, pos0):
    qb = pl.program_id(1)
    qpos = pos0 + qb * qb_rows + lax.broadcasted_iota(jnp.int32, (qb_rows, 1), 0)
    kt_top = (pos0 + qb * qb_rows + qb_rows - 1) // LANES
    lane1 = lax.broadcasted_iota(jnp.int32, (1, LANES), 1)
    jj = lax.broadcasted_iota(jnp.int32, (LANES, 2 * LANES), 0)
    ss = lax.broadcasted_iota(jnp.int32, (LANES, 2 * LANES), 1)
    sum_rhs = jnp.where((jj > ss) | (ss >= LANES), 1.0, 0.0).astype(BF16)
    lane = lax.broadcasted_iota(jnp.int32, (qb_rows, LANES), 1)

    q = q_ref[0]
    q_heads = []
    for g in range(N_PAIRS):
        q_heads.extend(_head_halves(q[:, g * LANES:(g + 1) * LANES]))
    run_ref[...] = jnp.zeros(run_ref.shape, F32)
    acc_ref[...] = jnp.zeros(acc_ref.shape, F32)

    def cond(c):
        return c[0] > 0

    def body(c):
        _, kt = c
        rows = pl.ds(pl.multiple_of(kt * LANES, LANES), LANES)
        strict = (kt * LANES + lane1) < qpos
        heads = range(N_HEADS)
        k_t = [k_ref[0, rows, g * LANES:(g + 1) * LANES] for g in range(N_PAIRS)]
        v_t = [v_ref[0, rows, g * LANES:(g + 1) * LANES] for g in range(N_PAIRS)]
        z = [_nt_dot(q_heads[h], k_t[h // 2]) for h in heads]
        sp = [jnp.maximum(z[h], 0.0) + jnp.log1p(jnp.exp(-jnp.abs(z[h]))) for h in heads]
        ls = [jnp.where(strict, -sp[h], 0.0) for h in heads]
        ls_hi = [ls[h].astype(BF16) for h in heads]
        ls_lo = [(ls[h] - ls_hi[h].astype(F32)).astype(BF16) for h in heads]
        sums = [jnp.dot(ls_hi[h], sum_rhs, preferred_element_type=F32)
                + jnp.dot(ls_lo[h], sum_rhs, preferred_element_type=F32) for h in heads]
        run_old = [run_ref[h] for h in heads]
        wgt = [jnp.where(strict, jnp.exp(z[h] - sp[h] + run_old[h] + sums[h][:, :LANES]), 0.0)
               for h in heads]
        pv = [jnp.dot(wgt[h].astype(BF16), v_t[h // 2], preferred_element_type=F32) for h in heads]
        run_new = [run_old[h] + sums[h][:, LANES:] for h in heads]
        alive = run_new[0]
        for h in range(1, N_HEADS):
            alive = jnp.maximum(alive, run_new[h])
        for h in heads:
            acc_ref[h] += pv[h]
            run_ref[h] = run_new[h]
        go = jnp.where((kt > 0) & (jnp.max(alive) > SB_DEAD), 1, 0)
        return go, kt - 1

    lax.while_loop(cond, body, (jnp.int32(1), kt_top))
    for g in range(N_PAIRS):
        o_ref[0, :, g * LANES:(g + 1) * LANES] = jnp.where(
            lane < HEAD_DIM, acc_ref[2 * g], acc_ref[2 * g + 1]).astype(o_ref.dtype)


def _sb(qb16, k16, v16, *, pos0, qb_rows):
    bsz, tq, _ = qb16.shape
    l_pad = k16.shape[1]
    assert tq % qb_rows == 0 and l_pad % LANES == 0
    q_map = lambda b, i: (b, i, 0)
    all_map = lambda b, i: (b, 0, 0)
    resident = dict(pipeline_mode=pl.Buffered(1)) if bsz == 1 else {}
    return pl.pallas_call(
        functools.partial(_sb_kernel, qb_rows=qb_rows, pos0=pos0),
        grid=(bsz, tq // qb_rows),
        in_specs=[pl.BlockSpec((1, qb_rows, D_HEADS), q_map),
                  pl.BlockSpec((1, l_pad, D_HEADS), all_map, **resident),
                  pl.BlockSpec((1, l_pad, D_HEADS), all_map, **resident)],
        out_specs=pl.BlockSpec((1, qb_rows, D_HEADS), q_map),
        out_shape=jax.ShapeDtypeStruct((bsz, tq, D_HEADS), BF16),
        scratch_shapes=[pltpu.VMEM((N_HEADS, qb_rows, LANES), F32),
                        pltpu.VMEM((N_HEADS, qb_rows, LANES), F32)],
        compiler_params=pltpu.CompilerParams(dimension_semantics=("arbitrary", "arbitrary"),
                                             vmem_limit_bytes=VMEM_LIMIT),
        name="sb",
    )(qb16, k16, v16)


def _post_kernel(oa_ref, ob_ref, sga_ref, sgb_ref, x_ref, wa_ref, wb_ref, wo_ref, g_ref,
                 wr_hi_ref, wr_lo_ref, br_ref, h_o, hn_o, comb_o, *, tm):
    ya = jnp.dot(oa_ref[...], wa_ref[...], preferred_element_type=F32)
    yb = jnp.dot(ob_ref[...], wb_ref[...], preferred_element_type=F32)
    mix = sga_ref[...] * ya + sgb_ref[...] * yb
    h = x_ref[...] + jnp.dot(mix.astype(BF16), wo_ref[...], preferred_element_type=F32)
    h_o[...] = h
    r = lax.rsqrt(jnp.mean(h * h, axis=-1, keepdims=True) + RMS_EPS)
    hn = (h * r) * g_ref[...]
    hn_hi = hn.astype(BF16)
    hn_o[...] = hn_hi
    hn_lo = (hn - hn_hi.astype(F32)).astype(BF16)
    logits = (jnp.dot(hn_hi, wr_hi_ref[...], preferred_element_type=F32)
              + jnp.dot(hn_lo, wr_hi_ref[...], preferred_element_type=F32)
              + jnp.dot(hn_hi, wr_lo_ref[...], preferred_element_type=F32)) + br_ref[...]

    lane = lax.broadcasted_iota(jnp.int32, (tm, LANES), 1)
    is_g = (lane >= N_EXPERTS) & (lane < N_EXPERTS + N_GROUPS)
    gl = jnp.where(is_g, logits, -jnp.inf)
    gmax = jnp.max(gl, axis=1, keepdims=True)
    g_lane = jnp.min(jnp.where(gl == gmax, lane, 2 ** 30), axis=1, keepdims=True)
    g_w = 1.0 / jnp.sum(jnp.exp(gl - gmax), axis=1, keepdims=True)
    in_grp = (lane < N_EXPERTS) & ((lane // EXPERTS_PER_GROUP) == (g_lane - N_EXPERTS))
    e1 = jnp.where(in_grp, logits, -jnp.inf)
    v1 = jnp.max(e1, axis=1, keepdims=True)
    i1 = jnp.min(jnp.where(e1 == v1, lane, 2 ** 30), axis=1, keepdims=True)
    e2 = jnp.where(lane == i1, -jnp.inf, e1)
    v2 = jnp.max(e2, axis=1, keepdims=True)
    i2 = jnp.min(jnp.where(e2 == v2, lane, 2 ** 30), axis=1, keepdims=True)
    t2 = jnp.exp(v2 - v1)
    den = 1.0 + t2
    comb_o[...] = jnp.where(lane == i1, (1.0 / den) * g_w,
                            jnp.where(lane == i2, (t2 / den) * g_w, 0.0))


def _post(oa, ob, sga, sgb, x2d, wa16, wb16, wo16, g_ffn, wr_hi, wr_lo, br):
    n = x2d.shape[0]
    tm = min(256, n)
    assert n % tm == 0
    row = lambda i: (i, 0)
    fix = lambda i: (0, 0)
    return pl.pallas_call(
        functools.partial(_post_kernel, tm=tm),
        grid=(n // tm,),
        in_specs=[pl.BlockSpec((tm, D_HEADS), row), pl.BlockSpec((tm, D_HEADS), row),
                  pl.BlockSpec((tm, D_MODEL), row), pl.BlockSpec((tm, D_MODEL), row),
                  pl.BlockSpec((tm, D_MODEL), row),
                  pl.BlockSpec((D_HEADS, D_MODEL), fix), pl.BlockSpec((D_HEADS, D_MODEL), fix),
                  pl.BlockSpec((D_MODEL, D_MODEL), fix), pl.BlockSpec((1, D_MODEL), fix),
                  pl.BlockSpec((D_MODEL, LANES), fix), pl.BlockSpec((D_MODEL, LANES), fix),
                  pl.BlockSpec((1, LANES), fix)],
        out_specs=[pl.BlockSpec((tm, D_MODEL), row), pl.BlockSpec((tm, D_MODEL), row),
                   pl.BlockSpec((tm, LANES), row)],
        out_shape=[jax.ShapeDtypeStruct((n, D_MODEL), F32),
                   jax.ShapeDtypeStruct((n, D_MODEL), BF16),
                   jax.ShapeDtypeStruct((n, LANES), F32)],
        compiler_params=pltpu.CompilerParams(dimension_semantics=("arbitrary",),
                                             vmem_limit_bytes=VMEM_LIMIT),
        name="post",
    )(oa, ob, sga, sgb, x2d, wa16, wb16, wo16, g_ffn, wr_hi, wr_lo, br)


def _moe_kernel(hn_ref, comb_ref, h_ref, wg_ref, wu_ref, wd_ref, gf_ref, y_o, acc_ref, *, tm):
    e = pl.program_id(1)

    @pl.when(e == 0)
    def _init():
        acc_ref[...] = jnp.zeros(acc_ref.shape, F32)

    hn = hn_ref[...]
    gate = jnp.dot(hn, wg_ref[0].astype(BF16), preferred_element_type=F32)
    up = jnp.dot(hn, wu_ref[0].astype(BF16), preferred_element_type=F32)
    lane = lax.broadcasted_iota(jnp.int32, (tm, LANES), 1)
    c_e = jnp.sum(jnp.where(lane == e, comb_ref[...], 0.0), axis=1, keepdims=True)
    hid = (gate * jax.nn.sigmoid(gate)) * up * c_e
    acc_ref[...] += jnp.dot(hid.astype(BF16), wd_ref[0].astype(BF16), preferred_element_type=F32)

    @pl.when(e == N_EXPERTS - 1)
    def _finish():
        out = h_ref[...] + acc_ref[...]
        r = lax.rsqrt(jnp.mean(out * out, axis=-1, keepdims=True) + RMS_EPS)
        y_o[...] = (out * r) * gf_ref[...]


def _moe(hn16, comb, h, wg16, wu16, wd16, g_final):
    n = h.shape[0]
    tm = min(1024, n)
    assert n % tm == 0
    row = lambda i, e: (i, 0)
    exp = lambda i, e: (e, 0, 0)
    fix = lambda i, e: (0, 0)
    return pl.pallas_call(
        functools.partial(_moe_kernel, tm=tm),
        grid=(n // tm, N_EXPERTS),
        in_specs=[pl.BlockSpec((tm, D_MODEL), row), pl.BlockSpec((tm, LANES), row),
                  pl.BlockSpec((tm, D_MODEL), row),
                  pl.BlockSpec((1, D_MODEL, D_FF_EXPERT), exp),
                  pl.BlockSpec((1, D_MODEL, D_FF_EXPERT), exp),
                  pl.BlockSpec((1, D_FF_EXPERT, D_MODEL), exp),
                  pl.BlockSpec((1, D_MODEL), fix)],
        out_specs=pl.BlockSpec((tm, D_MODEL), row),
        out_shape=jax.ShapeDtypeStruct((n, D_MODEL), F32),
        scratch_shapes=[pltpu.VMEM((tm, D_MODEL), F32)],
        compiler_params=pltpu.CompilerParams(dimension_semantics=("arbitrary", "arbitrary"),
                                             vmem_limit_bytes=VMEM_LIMIT),
        name="moe",
    )(hn16, comb, h, wg16, wu16, wd16, g_final)


JOIN_ROWS = 512


def _join_kernel(*refs, n_arrays, n_cache_tiles, t_new):
    caches, news, outs = (refs[k * n_arrays:(k + 1) * n_arrays] for k in range(3))
    j = pl.program_id(1)

    @pl.when(j < n_cache_tiles)
    def _copy():
        for c, o in zip(caches, outs):
            o[0] = c[0].astype(BF16)

    @pl.when(j >= n_cache_tiles)
    def _tail():
        for nw, o in zip(news, outs):
            o[0] = jnp.zeros(o.shape[1:], BF16)
            o[0, :t_new, :] = nw[0]


def _join(caches, news, l_pad):
    bsz, p_len, width = caches[0].shape
    t_new = news[0].shape[1]
    n_arrays = len(caches)
    assert p_len % JOIN_ROWS == 0 and l_pad == p_len + JOIN_ROWS
    assert t_new <= JOIN_ROWS and t_new % 16 == 0
    n_cache_tiles = p_len // JOIN_ROWS
    cache_map = lambda b, j: (b, jnp.minimum(j, n_cache_tiles - 1), 0)
    new_map = lambda b, j: (b, 0, 0)
    out_map = lambda b, j: (b, j, 0)
    return pl.pallas_call(
        functools.partial(_join_kernel, n_arrays=n_arrays, n_cache_tiles=n_cache_tiles, t_new=t_new),
        grid=(bsz, l_pad // JOIN_ROWS),
        in_specs=([pl.BlockSpec((1, JOIN_ROWS, width), cache_map)] * n_arrays
                  + [pl.BlockSpec((1, t_new, width), new_map)] * n_arrays),
        out_specs=[pl.BlockSpec((1, JOIN_ROWS, width), out_map)] * n_arrays,
        out_shape=[jax.ShapeDtypeStruct((bsz, l_pad, width), BF16)] * n_arrays,
        compiler_params=pltpu.CompilerParams(dimension_semantics=("arbitrary", "arbitrary"),
                                             vmem_limit_bytes=VMEM_LIMIT),
        name="join",
    )(*caches, *news)


def _prep_weights(norm_mix_g, w_in, w_br_a, w_br_b, w_out, norm_ffn_g, w_rg, b_rg, w_re, b_re,
                  w_eg, w_eu, w_ed, norm_final_g):
    offs = np.cumsum(IN_SPLIT)[:-1].tolist()
    q_a, k_a, v_a, q_i, k_i, w_i, q_b, k_b, v_b, g_a, g_b = jnp.split(w_in, offs, axis=-1)
    w_i_pad = jnp.pad(w_i, ((0, 0), (0, LANES - N_IDX_HEADS)))
    w16 = jnp.concatenate([q_a, k_a, v_a, q_i, k_i, k_i, w_i_pad, q_b, k_b, v_b, g_a, g_b],
                          axis=1).astype(BF16)
    half = ROT_HALF
    inv_freq = ROPE_THETA ** (-jnp.arange(half, dtype=F32) / half)
    d = np.arange(LANES) % HEAD_DIM
    invf = jnp.where(jnp.asarray(d < ROT_DIM), inv_freq[jnp.asarray(d % half)], 0.0)[None, :]
    pad = LANES - N_EXPERTS - N_GROUPS
    w_r = jnp.pad(jnp.concatenate([w_re, w_rg], axis=1), ((0, 0), (0, pad)))
    wr_hi = w_r.astype(BF16)
    wr_lo = (w_r - wr_hi.astype(F32)).astype(BF16)
    b_r = jnp.pad(jnp.concatenate([b_re, b_rg]), (0, pad))[None, :]
    return dict(g_mix=norm_mix_g[None, :], w16=w16, invf=invf,
                wa16=w_br_a.astype(BF16), wb16=w_br_b.astype(BF16), wo16=w_out.astype(BF16),
                g_ffn=norm_ffn_g[None, :], wr_hi=wr_hi, wr_lo=wr_lo, b_r=b_r,
                wg=w_eg, wu=w_eu, wd=w_ed,
                g_final=norm_final_g[None, :])


def _layer(x, pos0, past, wts, *, dsa_qb, dsa_kt, dsa_ch, sb_qb):
    bsz, t, _ = x.shape
    n = bsz * t
    x2d = x.reshape(n, D_MODEL)
    transposed_dsa = dsa_qb % LANES == 0
    qa_scale = ATT_SCALE * LOG2_E if transposed_dsa else ATT_SCALE
    pr = _proj(x2d, wts["g_mix"], wts["w16"], wts["invf"], tq=t, pos0=pos0, qa_scale=qa_scale)
    shp = lambda a: a.reshape(bsz, t, a.shape[-1])

    if past is None:
        l_valid = t
        ka16, va16, ki16, kb16, vb16 = (shp(pr[k]) for k in ("ka16", "va16", "ki16", "kb16", "vb16"))
    else:
        p_len = past[0].shape[1]
        l_valid = p_len + t
        l_pad = -(-l_valid // dsa_kt) * dsa_kt
        flat = lambda c: c.reshape(bsz, p_len, -1)
        ka16, va16, kb16, vb16 = _join(
            [flat(past[i]) for i in (0, 1, 3, 4)],
            [shp(pr[k]) for k in ("ka16", "va16", "kb16", "vb16")], l_pad)
        ki_c = flat(past[2]).astype(BF16)
        ki16 = jnp.concatenate([jnp.concatenate([ki_c, ki_c], axis=-1), shp(pr["ki16"]),
                                jnp.zeros((bsz, l_pad - l_valid, LANES), BF16)], axis=1)

    if transposed_dsa:
        l_all = va16.shape[1]
        vt = va16.reshape(bsz, l_all, N_HEADS, HEAD_DIM).transpose(0, 2, 3, 1)
        vt = jnp.concatenate([vt, jnp.ones((bsz, N_HEADS, 1, l_all), BF16),
                              jnp.zeros((bsz, N_HEADS, LANES - HEAD_DIM - 1, l_all), BF16)], axis=2)
        oa = _dsat(shp(pr["qa"]), shp(pr["qi"]), shp(pr["wi"]), ki16, ka16,
                   vt.reshape(bsz, N_HEADS * LANES, l_all), pos0=pos0, l_valid=l_valid,
                   qb_rows=dsa_qb, kt_rows=dsa_kt)
    else:
        oa = _dsa(shp(pr["qa"]), shp(pr["qi"]), shp(pr["wi"]), ki16, ka16, va16, pos0=pos0,
                  l_valid=l_valid, qb_rows=dsa_qb, kt_rows=dsa_kt, ch=dsa_ch)
    ob = _sb(shp(pr["qb"]), kb16, vb16, pos0=pos0, qb_rows=sb_qb)
    h, hn16, comb = _post(oa.reshape(n, D_HEADS), ob.reshape(n, D_HEADS), pr["sga"], pr["sgb"],
                          x2d, wts["wa16"], wts["wb16"], wts["wo16"], wts["g_ffn"],
                          wts["wr_hi"], wts["wr_lo"], wts["b_r"])
    y = _moe(hn16, comb, h, wts["wg"], wts["wu"], wts["wd"], wts["g_final"])
    heads = lambda a: a.reshape(1, bsz, t, N_HEADS, HEAD_DIM)
    rows = (heads(pr["ka"]), heads(pr["va"]),
            pr["ki"][:, :IDX_DIM].reshape(1, bsz, t, IDX_DIM),
            heads(pr["kb"]), heads(pr["vb"]))
    return y.reshape(bsz, t, D_MODEL), rows


def kernel(x_prompt, x_sample, cache_a_k, cache_a_v, cache_idx_k, cache_b_k, cache_b_v,
           norm_mix_g, w_in, w_br_a, w_br_b, w_out, norm_ffn_g,
           w_router_group, b_router_group, w_router_expert, b_router_expert,
           w_exp_gate, w_exp_up, w_exp_down, norm_final_g):
    assert w_in.shape[0] == 1, "single-layer model"
    wts = _prep_weights(norm_mix_g[0], w_in[0], w_br_a[0], w_br_b[0], w_out[0], norm_ffn_g[0],
                        w_router_group[0], b_router_group[0], w_router_expert[0],
                        b_router_expert[0], w_exp_gate[0], w_exp_up[0], w_exp_down[0],
                        norm_final_g)
    y_p, rows_p = _layer(x_prompt, 0, None, wts, dsa_qb=256, dsa_kt=1024, dsa_ch=512, sb_qb=128)
    past = (cache_a_k[0], cache_a_v[0], cache_idx_k[0], cache_b_k[0], cache_b_v[0])
    p_len = cache_a_k.shape[2]
    t_s = x_sample.shape[1]
    l_s = -(-(p_len + t_s) // 512) * 512
    y_s, rows_s = _layer(x_sample, p_len, past, wts, dsa_qb=t_s, dsa_kt=l_s, dsa_ch=512, sb_qb=t_s)
    return (y_p, y_s) + rows_p + rows_s
```

```python
import functools

import numpy as np
import jax
import jax.numpy as jnp
from jax import lax
from jax.experimental import pallas as pl
from jax.experimental.pallas import tpu as pltpu

D_MODEL = 1024
HEAD_DIM = 64
N_HEADS = 8
D_HEADS = N_HEADS * HEAD_DIM
N_IDX_HEADS = 8
IDX_DIM = 64
IDX_SCALE = (N_IDX_HEADS * IDX_DIM) ** -0.5
ATT_SCALE = HEAD_DIM ** -0.5
LOG2_E = 1.4426950408889634
CHUNK = 64
TOPK_MAX = 256
ROPE_THETA = 500000.0
ROT_DIM = HEAD_DIM // 4
ROT_HALF = ROT_DIM // 2
N_GROUPS = 4
EXPERTS_PER_GROUP = 8
N_EXPERTS = N_GROUPS * EXPERTS_PER_GROUP
D_FF_EXPERT = D_MODEL // 4
RMS_EPS = 1e-6
IN_SPLIT = (D_HEADS, D_HEADS, D_HEADS, N_IDX_HEADS * IDX_DIM, IDX_DIM, N_IDX_HEADS,
            D_HEADS, D_HEADS, D_HEADS, D_MODEL, D_MODEL)

LANES = 128
SUBLANES = 8
N_PAIRS = N_HEADS // 2
NEG_BIG = -1e30
SB_DEAD = -110.0
VMEM_LIMIT = 56 * 1024 * 1024

C_QA, C_KA, C_VA, C_QI, C_KI, C_WI, C_QB, C_KB, C_VB, C_GA, C_GB, C_END = (
    0, 512, 1024, 1536, 2048, 2176, 2304, 2816, 3328, 3840, 4864, 5888)

F32 = jnp.float32
BF16 = jnp.bfloat16


def _nt_dot(a, b):
    return lax.dot_general(a, b, (((1,), (1,)), ((), ())), preferred_element_type=F32)


def _reduce0(x, op, ways=8):
    n = x.shape[0]
    if n % ways != 0 or n <= ways:
        ways = 1
    part = x.reshape(n // ways, ways, *x.shape[1:])
    acc = part[0]
    for j in range(1, n // ways):
        acc = op(acc, part[j])
    out = acc[0]
    for j in range(1, ways):
        out = op(out, acc[j])
    return out


def _head_halves(x):
    lane = lax.broadcasted_iota(jnp.int32, x.shape, 1)
    zero = jnp.zeros_like(x)
    return jnp.where(lane < HEAD_DIM, x, zero), jnp.where(lane >= HEAD_DIM, x, zero)


def _proj_kernel(x_ref, g_ref, w_ref, invf_ref,
                 qa_o, qi_o, qb_o, ka_o, va_o, kb_o, vb_o, ki_o, wi_o,
                 ka16_o, va16_o, kb16_o, vb16_o, ki16_o, sga_o, sgb_o,
                 *, tm, tq, pos0, qa_scale):
    x = x_ref[...]
    r = lax.rsqrt(jnp.mean(x * x, axis=-1, keepdims=True) + RMS_EPS)
    xn = (x * r) * g_ref[...]
    p = jnp.dot(xn.astype(BF16), w_ref[...], preferred_element_type=F32)

    row = pl.program_id(0) * tm + lax.broadcasted_iota(jnp.int32, (tm, 1), 0)
    pos = (row % tq + pos0).astype(F32)
    ang = pos * invf_ref[...]
    c = jnp.cos(ang)
    s = jnp.sin(ang)
    d = lax.broadcasted_iota(jnp.int32, (1, LANES), 1) % HEAD_DIM
    s_lo = jnp.where(d < ROT_HALF, -s, 0.0)
    s_hi = jnp.where((d >= ROT_HALF) & (d < ROT_DIM), s, 0.0)

    def rope(v):
        return (v * c + pltpu.roll(v, LANES - ROT_HALF, 1) * s_lo
                + pltpu.roll(v, ROT_HALF, 1) * s_hi)

    for j in range(D_HEADS // LANES):
        sl = slice(j * LANES, (j + 1) * LANES)
        qa = rope(p[:, C_QA + j * LANES:C_QA + (j + 1) * LANES])
        qa_o[:, sl] = (qa * qa_scale).astype(BF16)
        ka = rope(p[:, C_KA + j * LANES:C_KA + (j + 1) * LANES])
        ka_o[:, sl] = ka
        ka16_o[:, sl] = ka.astype(BF16)
        qi = rope(p[:, C_QI + j * LANES:C_QI + (j + 1) * LANES])
        qi_o[:, sl] = qi.astype(BF16)
    ki = rope(p[:, C_KI:C_KI + LANES])
    ki_o[...] = ki
    ki16_o[...] = ki.astype(BF16)
    wi_o[...] = p[:, C_WI:C_WI + LANES]
    va = p[:, C_VA:C_VA + D_HEADS]
    va_o[...] = va
    va16_o[...] = va.astype(BF16)
    qb_o[...] = (p[:, C_QB:C_QB + D_HEADS] * ATT_SCALE).astype(BF16)
    kb = p[:, C_KB:C_KB + D_HEADS]
    kb_o[...] = kb
    kb16_o[...] = kb.astype(BF16)
    vb = p[:, C_VB:C_VB + D_HEADS]
    vb_o[...] = vb
    vb16_o[...] = vb.astype(BF16)
    sga_o[...] = jax.nn.sigmoid(p[:, C_GA:C_GA + D_MODEL])
    sgb_o[...] = jax.nn.sigmoid(p[:, C_GB:C_GB + D_MODEL])


def _proj(x2d, g, w16, invf, *, tq, pos0, qa_scale):
    n = x2d.shape[0]
    tm = min(256, n)
    assert n % tm == 0
    row = lambda i: (i, 0)
    fix = lambda i: (0, 0)
    widths = dict(qa=D_HEADS, qi=D_HEADS, qb=D_HEADS, ka=D_HEADS, va=D_HEADS, kb=D_HEADS,
                  vb=D_HEADS, ki=LANES, wi=LANES, ka16=D_HEADS, va16=D_HEADS, kb16=D_HEADS,
                  vb16=D_HEADS, ki16=LANES, sga=D_MODEL, sgb=D_MODEL)
    dtypes = dict(qa=BF16, qi=BF16, qb=BF16, ka16=BF16, va16=BF16, kb16=BF16, vb16=BF16, ki16=BF16)
    names = list(widths)
    out_shape = [jax.ShapeDtypeStruct((n, widths[k]), dtypes.get(k, F32)) for k in names]
    out_specs = [pl.BlockSpec((tm, widths[k]), row) for k in names]
    outs = pl.pallas_call(
        functools.partial(_proj_kernel, tm=tm, tq=tq, pos0=pos0, qa_scale=qa_scale),
        grid=(n // tm,),
        in_specs=[pl.BlockSpec((tm, D_MODEL), row),
                  pl.BlockSpec((1, D_MODEL), fix),
                  pl.BlockSpec((D_MODEL, C_END), fix),
                  pl.BlockSpec((1, LANES), fix)],
        out_specs=out_specs,
        out_shape=out_shape,
        compiler_params=pltpu.CompilerParams(dimension_semantics=("arbitrary",),
                                             vmem_limit_bytes=VMEM_LIMIT),
        name="proj",
    )(x2d, g, w16, invf)
    return dict(zip(names, outs))


def _dsac_kernel(qa_ref, qi_ref, wi_ref, kic_ref, kin_ref, kc_ref, vc_ref, kn_ref, vn_ref, o_ref,
                 score_ref, thr_ref, jcut_ref, m_ref, l_ref, acc_ref,
                 *, n_ct, kt_rows, t_new, p_len, topk):
    kt = pl.program_id(1)
    qb_rows = t_new
    ch = kt_rows
    n_ch = n_ct + 1
    l_valid = p_len + t_new

    row = lax.broadcasted_iota(jnp.int32, (qb_rows, 1), 0)
    limit = jnp.minimum(((p_len + row) // CHUNK + 1) * CHUNK, l_valid)

    def pad_keys(x):
        return jnp.concatenate([x, jnp.zeros((LANES - t_new, HEAD_DIM), x.dtype)], axis=0)

    def count_where(pred_fn):
        def body(t, acc):
            blk = score_ref[t]
            for c in range(ch // LANES):
                acc = acc + jnp.where(pred_fn(t, blk[:, c * LANES:(c + 1) * LANES], c), 1.0, 0.0)
            return acc
        acc = lax.fori_loop(0, n_ch, body, jnp.zeros((qb_rows, LANES), F32))
        return jnp.sum(acc, axis=1, keepdims=True)

    @pl.when(kt == 0)
    def _select():
        qi = qi_ref[0]
        q_idx = [qi[:, h * IDX_DIM:(h + 1) * IDX_DIM] for h in range(N_IDX_HEADS)]
        w = wi_ref[0] * IDX_SCALE
        w_cols = [w[:, h:h + 1] for h in range(N_IDX_HEADS)]

        def scores(kk):
            s = jnp.zeros((qb_rows, kk.shape[0]), F32)
            for h in range(N_IDX_HEADS):
                s = s + w_cols[h] * jnp.maximum(_nt_dot(q_idx[h], kk), 0.0)
            return s

        def score_body(t, carry):
            kk = kic_ref[0, pl.ds(pl.multiple_of(t * ch, ch), ch), :].astype(BF16)
            kpos = t * ch + lax.broadcasted_iota(jnp.int32, (1, ch), 1)
            score_ref[t] = jnp.where(kpos < limit, scores(kk), -jnp.inf)
            return carry
        lax.fori_loop(0, n_ct, score_body, 0)
        s_new = scores(pad_keys(kin_ref[0][:, :IDX_DIM]))
        kpos = p_len + lax.broadcasted_iota(jnp.int32, (1, LANES), 1)
        score_ref[n_ct] = jnp.concatenate(
            [jnp.where(kpos < limit, s_new, -jnp.inf),
             jnp.full((qb_rows, ch - LANES), -jnp.inf, F32)], axis=1)

        def minmax_body(t, carry):
            mn, mx = carry
            blk = score_ref[t]
            for c in range(ch // LANES):
                piece = blk[:, c * LANES:(c + 1) * LANES]
                mx = jnp.maximum(mx, piece)
                mn = jnp.minimum(mn, jnp.where(piece == -jnp.inf, jnp.inf, piece))
            return mn, mx
        mn, mx = lax.fori_loop(0, n_ch, minmax_body,
                               (jnp.full((qb_rows, LANES), jnp.inf, F32),
                                jnp.full((qb_rows, LANES), -jnp.inf, F32)))
        lo0 = jnp.min(mn, axis=1, keepdims=True)
        smax = jnp.max(mx, axis=1, keepdims=True)
        hi0 = smax + (jnp.abs(smax) * 1e-6 + 1e-30)
        all_in = limit <= topk
        done0 = jnp.where(all_in, 1.0, 0.0)
        thr0 = jnp.where(all_in, jnp.float32(-3e38), lo0)
        zeros = jnp.zeros((qb_rows, 1), F32)

        def cond(c):
            return c[0] > 0

        def bisect_pass(lo, hi, chi, thr, done, tie, cand, pend):
            half = lo + 0.5 * (hi - lo)
            has_cand = pend > 0.5
            mid = jnp.where(has_cand, cand, half)
            stuck = jnp.logical_not(has_cand) & ((half <= lo) | (half >= hi))
            mid_b = jnp.broadcast_to(mid, (qb_rows, LANES))
            cnt = count_where(lambda t, piece, cidx: piece >= mid_b)
            active = done < 0.5
            moving = active & jnp.logical_not(stuck)
            found = moving & ((cnt == topk) | (has_cand & (cnt > topk)))
            now_tie = (active & stuck) | (moving & has_cand & (cnt > topk))
            up = moving & jnp.logical_not(has_cand) & (cnt > topk)
            down = moving & (cnt < topk)
            thr = jnp.where(found, mid, jnp.where(active & stuck, lo, thr))
            tie = jnp.where(now_tie, 1.0, tie)
            done = jnp.where(found | now_tie, 1.0, done)
            lo = jnp.where(up, mid, lo)
            chi = jnp.where(down, cnt, chi)
            hi = jnp.where(down, mid, hi)
            return lo, hi, chi, thr, done, tie, cand, jnp.zeros_like(pend)

        def snap_pass(lo, hi, chi, thr, done, tie, cand, pend):
            hi_b = jnp.broadcast_to(hi, (qb_rows, LANES))

            def body(t, acc):
                blk = score_ref[t]
                for c in range(ch // LANES):
                    piece = blk[:, c * LANES:(c + 1) * LANES]
                    acc = jnp.maximum(acc, jnp.where(piece < hi_b, piece, -jnp.inf))
                return acc
            acc = lax.fori_loop(0, n_ch, body, jnp.full((qb_rows, LANES), -jnp.inf, F32))
            cand = jnp.max(acc, axis=1, keepdims=True)
            return lo, hi, chi, thr, done, tie, cand, 1.0 - done

        def body(c):
            it = c[1]
            snap = (it >= 11) & (it % 4 == 3)
            new = lax.cond(snap, snap_pass, bisect_pass, *c[2:])
            left = jnp.sum(1.0 - new[4])
            go = jnp.where((left > 0) & (it < 400), 1, 0)
            return (go, it + 1) + tuple(new)

        left0 = jnp.sum(1.0 - done0)
        state = (jnp.where(left0 > 0, 1, 0), jnp.int32(0), lo0, hi0, zeros, thr0, done0, zeros,
                 zeros, zeros)
        _, _, _, _, chi, thr, _, tie, _, _ = lax.while_loop(cond, body, state)

        big = jnp.full((qb_rows, 1), 2 ** 30, jnp.int32)

        def tie_break():
            need = topk - chi
            thr_b = jnp.broadcast_to(thr, (qb_rows, LANES))
            lane1 = lax.broadcasted_iota(jnp.int32, (1, LANES), 1)

            def tb_body(_, c):
                lo_j, hi_j = c
                mid_j = lo_j + (hi_j - lo_j) // 2
                mid_jb = jnp.broadcast_to(mid_j, (qb_rows, LANES))
                cnt = count_where(lambda t, piece, cidx: (piece == thr_b)
                                  & (t * ch + cidx * LANES + lane1 <= mid_jb))
                ok = cnt >= need
                return jnp.where(ok, lo_j, mid_j), jnp.where(ok, mid_j, hi_j)

            lo_j = jnp.full((qb_rows, 1), -1, jnp.int32)
            hi_j = jnp.zeros((qb_rows, 1), jnp.int32) + (n_ch * ch - 1)
            n_steps = int(np.ceil(np.log2(score_ref.shape[0] * ch))) + 1
            _, hi_j = lax.fori_loop(0, n_steps, tb_body, (lo_j, hi_j))
            return jnp.where(tie > 0, hi_j, big)

        jcut = lax.cond(jnp.sum(tie) > 0, tie_break, lambda: big)
        thr_ref[...] = jnp.broadcast_to(thr, (qb_rows, LANES))
        jcut_ref[...] = jnp.broadcast_to(jcut, (qb_rows, LANES))
        m_ref[...] = jnp.full(m_ref.shape, NEG_BIG, F32)
        l_ref[...] = jnp.zeros(l_ref.shape, F32)
        acc_ref[...] = jnp.zeros(acc_ref.shape, F32)

    def update(n_keys, k_of_pair, v_of_pair):
        thr_b = thr_ref[...]
        jcut_b = jcut_ref[...]
        lane1 = lax.broadcasted_iota(jnp.int32, (1, LANES), 1)
        blk = score_ref[kt]
        bias = []
        for c in range(n_keys // LANES):
            piece = blk[:, c * LANES:(c + 1) * LANES]
            kpos = kt * kt_rows + c * LANES + lane1
            sel = (piece >= thr_b) & ((piece > thr_b) | (kpos <= jcut_b))
            bias.append(jnp.where(sel, 0.0, NEG_BIG))
        n_piece = len(bias)
        qa = qa_ref[0]
        for g in range(N_PAIRS):
            k_g = k_of_pair(g)
            v_g = v_of_pair(g)
            for half, q_h in enumerate(_head_halves(qa[:, g * LANES:(g + 1) * LANES])):
                h = 2 * g + half
                logits = _nt_dot(q_h, k_g)
                pieces = [logits[:, c * LANES:(c + 1) * LANES] + bias[c] for c in range(n_piece)]
                mx = pieces[0]
                for c in range(1, n_piece):
                    mx = jnp.maximum(mx, pieces[c])
                m_old = m_ref[h]
                m_new = jnp.maximum(m_old, jnp.max(mx, axis=1, keepdims=True))
                alpha = jnp.exp(m_old - m_new)
                ps = [jnp.exp(pc - m_new) for pc in pieces]
                psum = ps[0]
                for c in range(1, n_piece):
                    psum = psum + ps[c]
                p16 = jnp.concatenate([pc.astype(BF16) for pc in ps], axis=1)
                pv = jnp.dot(p16, v_g, preferred_element_type=F32)
                m_ref[h] = m_new
                l_ref[h] = alpha * l_ref[h] + psum
                acc_ref[h] = alpha * acc_ref[h] + pv

    @pl.when(kt < n_ct)
    def _cache_tile():
        update(kt_rows,
               lambda g: kc_ref[0, :, g * LANES:(g + 1) * LANES],
               lambda g: vc_ref[0, :, g * LANES:(g + 1) * LANES])

    @pl.when(kt == n_ct)
    def _new_tile():
        def pad_pair(x):
            return jnp.concatenate([x, jnp.zeros((LANES - t_new, LANES), x.dtype)], axis=0)
        kn = kn_ref[0]
        vn = vn_ref[0]
        update(LANES,
               lambda g: pad_pair(kn[:, g * LANES:(g + 1) * LANES]),
               lambda g: pad_pair(vn[:, g * LANES:(g + 1) * LANES]))
        lane = lax.broadcasted_iota(jnp.int32, (qb_rows, LANES), 1)
        for g in range(N_PAIRS):
            outs = [acc_ref[h] / jnp.sum(l_ref[h], axis=1, keepdims=True) for h in (2 * g, 2 * g + 1)]
            o_ref[0, :, g * LANES:(g + 1) * LANES] = jnp.where(
                lane < HEAD_DIM, outs[0], outs[1]).astype(o_ref.dtype)


def _dsat_kernel(qb_of, kt_of, first_of, last_of,
                 qa_ref, qi_ref, wi_ref, ki_ref, k_ref, vt_ref, o_ref,
                 score_ref, m_ref, acc_ref,
                 *, qb_rows, kt_rows, pos0, l_valid, topk):
    i = pl.program_id(1)
    qb = qb_of[i]
    kt = kt_of[i]
    nq = qb_rows
    ns = kt_rows // SUBLANES

    qcol = qb * nq + lax.broadcasted_iota(jnp.int32, (SUBLANES, nq), 1)
    limit = jnp.minimum(((pos0 + qcol) // CHUNK + 1) * CHUNK, l_valid)
    max_limit = jnp.minimum(((pos0 + qb * nq + nq - 1) // CHUNK + 1) * CHUNK, l_valid)
    n_kt = (max_limit + kt_rows - 1) // kt_rows
    kidx = (lax.broadcasted_iota(jnp.int32, (ns, SUBLANES, nq), 0) * SUBLANES
            + lax.broadcasted_iota(jnp.int32, (ns, SUBLANES, nq), 1))

    def reduce_tiles(fn, op, init):
        def body(t, acc):
            return op(acc, fn(t, score_ref[t]))
        return lax.fori_loop(0, n_kt, body, init)

    def count_where(pred):
        acc = reduce_tiles(lambda t, blk: _reduce0(jnp.where(pred(t, blk), 1.0, 0.0), jnp.add),
                           jnp.add, jnp.zeros((SUBLANES, nq), F32))
        return jnp.broadcast_to(jnp.sum(acc, axis=0, keepdims=True), (SUBLANES, nq))

    @pl.when(first_of[i] == 1)
    def _select():
        qi = qi_ref[0]
        q_halves = []
        for g in range(N_PAIRS):
            q_halves.extend(_head_halves(qi[:, g * LANES:(g + 1) * LANES]))
        w_t = wi_ref[0].T * IDX_SCALE
        w_rows = [jnp.broadcast_to(w_t[h:h + 1, :], (SUBLANES, nq)) for h in range(N_IDX_HEADS)]

        def score_body(t, carry):
            mn, mx = carry
            kk = ki_ref[0, pl.ds(pl.multiple_of(t * kt_rows, kt_rows), kt_rows), :]
            s = jnp.zeros((ns, SUBLANES, nq), F32)
            for h in range(N_IDX_HEADS):
                rel = jnp.maximum(_nt_dot(kk, q_halves[h]), 0.0)
                s = s + w_rows[h] * rel.reshape(ns, SUBLANES, nq)
            seen = kidx + t * kt_rows < limit
            score_ref[t] = jnp.where(seen, s, -jnp.inf)
            mn = jnp.minimum(mn, _reduce0(jnp.where(seen, s, jnp.inf), jnp.minimum))
            mx = jnp.maximum(mx, _reduce0(jnp.where(seen, s, -jnp.inf), jnp.maximum))
            return mn, mx
        mn8, mx8 = lax.fori_loop(0, n_kt, score_body,
                                 (jnp.full((SUBLANES, nq), jnp.inf, F32),
                                  jnp.full((SUBLANES, nq), -jnp.inf, F32)))

        rep = lambda v: jnp.broadcast_to(v, (SUBLANES, nq))
        lo0 = rep(jnp.min(mn8, axis=0, keepdims=True))
        smax = rep(jnp.max(mx8, axis=0, keepdims=True))
        hi0 = smax + (jnp.abs(smax) * 1e-6 + 1e-30)
        all_in = limit <= topk
        done0 = jnp.where(all_in, 1.0, 0.0)
        thr0 = jnp.where(all_in, jnp.float32(-3e38), lo0)
        zeros = jnp.zeros((SUBLANES, nq), F32)

        def bisect_pass(lo, hi, chi, thr, done, tie, cand, pend):
            half = lo + 0.5 * (hi - lo)
            has_cand = pend > 0.5
            mid = jnp.where(has_cand, cand, half)
            stuck = jnp.logical_not(has_cand) & ((half <= lo) | (half >= hi))
            cnt = count_where(lambda t, blk: blk >= mid)
            active = done < 0.5
            moving = active & jnp.logical_not(stuck)
            found = moving & ((cnt == topk) | (has_cand & (cnt > topk)))
            now_tie = (active & stuck) | (moving & has_cand & (cnt > topk))
            up = moving & jnp.logical_not(has_cand) & (cnt > topk)
            down = moving & (cnt < topk)
            thr = jnp.where(found, mid, jnp.where(active & stuck, lo, thr))
            tie = jnp.where(now_tie, 1.0, tie)
            done = jnp.where(found | now_tie, 1.0, done)
            lo = jnp.where(up, mid, lo)
            chi = jnp.where(down, cnt, chi)
            hi = jnp.where(down, mid, hi)
            return lo, hi, chi, thr, done, tie, cand, jnp.zeros_like(pend)

        def snap_pass(lo, hi, chi, thr, done, tie, cand, pend):
            below = reduce_tiles(
                lambda t, blk: _reduce0(jnp.where(blk < hi, blk, -jnp.inf), jnp.maximum),
                jnp.maximum, jnp.full((SUBLANES, nq), -jnp.inf, F32))
            cand = rep(jnp.max(below, axis=0, keepdims=True))
            return lo, hi, chi, thr, done, tie, cand, 1.0 - done

        def cond(c):
            return c[0] > 0

        def body(c):
            it = c[1]
            snap = (it >= 11) & (it % 4 == 3)
            new = lax.cond(snap, snap_pass, bisect_pass, *c[2:])
            left = jnp.sum(1.0 - new[4])
            go = jnp.where((left > 0) & (it < 400), 1, 0)
            return (go, it + 1) + tuple(new)

        left0 = jnp.sum(1.0 - done0)
        state = (jnp.where(left0 > 0, 1, 0), jnp.int32(0), lo0, hi0, zeros, thr0, done0, zeros,
                 zeros, zeros)
        _, _, _, _, chi, thr, _, tie, _, _ = lax.while_loop(cond, body, state)

        big = jnp.full((SUBLANES, nq), 2 ** 30, jnp.int32)

        def tie_break():
            need = topk - chi

            def tb_body(_, c):
                lo_j, hi_j = c
                mid_j = lo_j + (hi_j - lo_j) // 2
                cnt = count_where(lambda t, blk: (blk == thr) & (kidx + t * kt_rows <= mid_j))
                ok = cnt >= need
                return jnp.where(ok, lo_j, mid_j), jnp.where(ok, mid_j, hi_j)

            lo_j = jnp.full((SUBLANES, nq), -1, jnp.int32)
            hi_j = jnp.zeros((SUBLANES, nq), jnp.int32) + (n_kt * kt_rows - 1)
            n_steps = int(np.ceil(np.log2(score_ref.shape[0] * kt_rows))) + 1
            _, hi_j = lax.fori_loop(0, n_steps, tb_body, (lo_j, hi_j))
            return jnp.where(tie > 0, hi_j, big)

        jcut = lax.cond(jnp.sum(tie) > 0, tie_break, lambda: big)

        def mask_body(t, carry):
            blk = score_ref[t]
            sel = (blk >= thr) & ((blk > thr) | (kidx + t * kt_rows <= jcut))
            score_ref[t] = jnp.where(sel, 0.0, NEG_BIG)
            return carry
        lax.fori_loop(0, n_kt, mask_body, 0)

        m_ref[...] = jnp.full(m_ref.shape, NEG_BIG, F32)
        acc_ref[...] = jnp.zeros(acc_ref.shape, F32)

    bias = score_ref[kt]
    qa = qa_ref[0]
    q_heads = []
    for g in range(N_PAIRS):
        q_heads.extend(_head_halves(qa[:, g * LANES:(g + 1) * LANES]))
    for h0 in range(0, N_HEADS, 4):
        heads = range(h0, h0 + 4)
        logits = {h: _nt_dot(k_ref[0, :, (h // 2) * LANES:(h // 2 + 1) * LANES], q_heads[h])
                  .reshape(ns, SUBLANES, nq) + bias for h in heads}
        m_old = {h: m_ref[h] for h in heads}
        m_new = {h: jnp.maximum(m_old[h], jnp.max(_reduce0(logits[h], jnp.maximum),
                                                  axis=0, keepdims=True)) for h in heads}
        alpha = {h: jnp.exp2(m_old[h] - m_new[h]) for h in heads}
        p = {h: jnp.exp2(logits[h] - m_new[h]) for h in heads}
        pv = {h: jnp.dot(vt_ref[0, h * LANES:(h + 1) * LANES, :],
                         p[h].reshape(kt_rows, nq).astype(BF16), preferred_element_type=F32)
              for h in heads}
        for h in heads:
            acc_ref[h] = alpha[h][0:1] * acc_ref[h] + pv[h]
            m_ref[h] = m_new[h]

    @pl.when(last_of[i] == 1)
    def _finish():
        for g in range(N_PAIRS):
            outs = []
            for h in (2 * g, 2 * g + 1):
                acc = acc_ref[h]
                outs.append(acc[:HEAD_DIM] / acc[HEAD_DIM:HEAD_DIM + 1])
            o_t = jnp.concatenate(outs, axis=0)
            o_ref[0, :, g * LANES:(g + 1) * LANES] = o_t.T.astype(o_ref.dtype)


def _dsat(qa, qi, wi, ki16, k16, vt16, *, pos0, l_valid, qb_rows, kt_rows):
    bsz, tq, _ = qa.shape
    l_pad = k16.shape[1]
    assert tq % qb_rows == 0 and l_pad % kt_rows == 0 and qb_rows % LANES == 0
    topk = min(TOPK_MAX, l_valid // 4)
    tables = _dsa_tables(tq, qb_rows, kt_rows, pos0, l_valid)
    q_map = lambda b, i, qb_of, kt_of, f, l: (b, qb_of[i], 0)
    k_map = lambda b, i, qb_of, kt_of, f, l: (b, kt_of[i], 0)
    vt_map = lambda b, i, qb_of, kt_of, f, l: (b, 0, kt_of[i])
    all_map = lambda b, i, qb_of, kt_of, f, l: (b, 0, 0)
    grid_spec = pltpu.PrefetchScalarGridSpec(
        num_scalar_prefetch=4,
        grid=(bsz, tables[0].shape[0]),
        in_specs=[pl.BlockSpec((1, qb_rows, D_HEADS), q_map),
                  pl.BlockSpec((1, qb_rows, D_HEADS), q_map),
                  pl.BlockSpec((1, qb_rows, LANES), q_map),
                  pl.BlockSpec((1, l_pad, LANES), all_map),
                  pl.BlockSpec((1, kt_rows, D_HEADS), k_map),
                  pl.BlockSpec((1, N_HEADS * LANES, kt_rows), vt_map)],
        out_specs=pl.BlockSpec((1, qb_rows, D_HEADS), q_map),
        scratch_shapes=[pltpu.VMEM((l_pad // kt_rows, kt_rows // SUBLANES, SUBLANES, qb_rows), F32),
                        pltpu.VMEM((N_HEADS, SUBLANES, qb_rows), F32),
                        pltpu.VMEM((N_HEADS, LANES, qb_rows), F32)])
    return pl.pallas_call(
        functools.partial(_dsat_kernel, qb_rows=qb_rows, kt_rows=kt_rows, pos0=pos0,
                          l_valid=l_valid, topk=topk),
        grid_spec=grid_spec,
        out_shape=jax.ShapeDtypeStruct((bsz, tq, D_HEADS), BF16),
        compiler_params=pltpu.CompilerParams(dimension_semantics=("arbitrary", "arbitrary"),
                                             vmem_limit_bytes=VMEM_LIMIT),
        name="dsat",
    )(*tables, qa, qi, wi, ki16, k16, vt16)


def _dsa_tables(tq, qb_rows, kt_rows, pos0, l_valid):
    qb_l, kt_l, first_l, last_l = [], [], [], []
    for qb in range(tq // qb_rows):
        max_limit = min(((pos0 + qb * qb_rows + qb_rows - 1) // CHUNK + 1) * CHUNK, l_valid)
        n_kt = -(-max_limit // kt_rows)
        for kt in range(n_kt):
            qb_l.append(qb)
            kt_l.append(kt)
            first_l.append(int(kt == 0))
            last_l.append(int(kt == n_kt - 1))
    return [jnp.asarray(np.asarray(a, np.int32)) for a in (qb_l, kt_l, first_l, last_l)]


def _dsac(qa, qi, wi, ki_cache, ki_new, k_cache, v_cache, k_new, v_new, *, kt_rows):
    bsz, t_new, _ = qa.shape
    p_len = k_cache.shape[1]
    assert p_len % kt_rows == 0 and kt_rows % LANES == 0 and kt_rows > LANES
    assert t_new % 16 == 0 and t_new <= LANES
    n_ct = p_len // kt_rows
    topk = min(TOPK_MAX, (p_len + t_new) // 4)
    q_map = lambda b, j: (b, 0, 0)
    c_map = lambda b, j: (b, jnp.minimum(j, n_ct - 1), 0)
    return pl.pallas_call(
        functools.partial(_dsac_kernel, n_ct=n_ct, kt_rows=kt_rows, t_new=t_new, p_len=p_len,
                          topk=topk),
        grid=(bsz, n_ct + 1),
        in_specs=[pl.BlockSpec((1, t_new, D_HEADS), q_map),
                  pl.BlockSpec((1, t_new, D_HEADS), q_map),
                  pl.BlockSpec((1, t_new, LANES), q_map),
                  pl.BlockSpec((1, p_len, IDX_DIM), q_map),
                  pl.BlockSpec((1, t_new, LANES), q_map),
                  pl.BlockSpec((1, kt_rows, D_HEADS), c_map),
                  pl.BlockSpec((1, kt_rows, D_HEADS), c_map),
                  pl.BlockSpec((1, t_new, D_HEADS), q_map),
                  pl.BlockSpec((1, t_new, D_HEADS), q_map)],
        out_specs=pl.BlockSpec((1, t_new, D_HEADS), q_map),
        out_shape=jax.ShapeDtypeStruct((bsz, t_new, D_HEADS), BF16),
        scratch_shapes=[pltpu.VMEM((n_ct + 1, t_new, kt_rows), F32),
                        pltpu.VMEM((t_new, LANES), F32),
                        pltpu.VMEM((t_new, LANES), jnp.int32),
                        pltpu.VMEM((N_HEADS, t_new, LANES), F32),
                        pltpu.VMEM((N_HEADS, t_new, LANES), F32),
                        pltpu.VMEM((N_HEADS, t_new, LANES), F32)],
        compiler_params=pltpu.CompilerParams(dimension_semantics=("arbitrary", "arbitrary"),
                                             vmem_limit_bytes=VMEM_LIMIT),
        name="dsac",
    )(qa, qi, wi, ki_cache, ki_new, k_cache, v_cache, k_new, v_new)


def _sb_kernel(q_ref, k_ref, v_ref, o_ref, run_ref, acc_ref, *, qb_rows, pos0):
    qb = pl.program_id(1)
    qpos = pos0 + qb * qb_rows + lax.broadcasted_iota(jnp.int32, (qb_rows, 1), 0)
    kt_top = (pos0 + qb * qb_rows + qb_rows - 1) // LANES
    lane1 = lax.broadcasted_iota(jnp.int32, (1, LANES), 1)
    jj = lax.broadcasted_iota(jnp.int32, (LANES, 2 * LANES), 0)
    ss = lax.broadcasted_iota(jnp.int32, (LANES, 2 * LANES), 1)
    sum_rhs = jnp.where((jj > ss) | (ss >= LANES), 1.0, 0.0).astype(BF16)
    lane = lax.broadcasted_iota(jnp.int32, (qb_rows, LANES), 1)

    q = q_ref[0]
    q_heads = []
    for g in range(N_PAIRS):
        q_heads.extend(_head_halves(q[:, g * LANES:(g + 1) * LANES]))
    run_ref[...] = jnp.zeros(run_ref.shape, F32)
    acc_ref[...] = jnp.zeros(acc_ref.shape, F32)

    def cond(c):
        return c[0] > 0

    def body(c):
        _, kt = c
        rows = pl.ds(pl.multiple_of(kt * LANES, LANES), LANES)
        strict = (kt * LANES + lane1) < qpos
        heads = range(N_HEADS)
        k_t = [k_ref[0, rows, g * LANES:(g + 1) * LANES] for g in range(N_PAIRS)]
        v_t = [v_ref[0, rows, g * LANES:(g + 1) * LANES] for g in range(N_PAIRS)]
        z = [_nt_dot(q_heads[h], k_t[h // 2]) for h in heads]
        sp = [jnp.maximum(z[h], 0.0) + jnp.log1p(jnp.exp(-jnp.abs(z[h]))) for h in heads]
        ls = [jnp.where(strict, -sp[h], 0.0) for h in heads]
        ls_hi = [ls[h].astype(BF16) for h in heads]
        ls_lo = [(ls[h] - ls_hi[h].astype(F32)).astype(BF16) for h in heads]
        sums = [jnp.dot(ls_hi[h], sum_rhs, preferred_element_type=F32)
                + jnp.dot(ls_lo[h], sum_rhs, preferred_element_type=F32) for h in heads]
        run_old = [run_ref[h] for h in heads]
        wgt = [jnp.where(strict, jnp.exp(z[h] - sp[h] + run_old[h] + sums[h][:, :LANES]), 0.0)
               for h in heads]
        pv = [jnp.dot(wgt[h].astype(BF16), v_t[h // 2], preferred_element_type=F32) for h in heads]
        run_new = [run_old[h] + sums[h][:, LANES:] for h in heads]
        alive = run_new[0]
        for h in range(1, N_HEADS):
            alive = jnp.maximum(alive, run_new[h])
        for h in heads:
            acc_ref[h] += pv[h]
            run_ref[h] = run_new[h]
        go = jnp.where((kt > 0) & (jnp.max(alive) > SB_DEAD), 1, 0)
        return go, kt - 1

    lax.while_loop(cond, body, (jnp.int32(1), kt_top))
    for g in range(N_PAIRS):
        o_ref[0, :, g * LANES:(g + 1) * LANES] = jnp.where(
            lane < HEAD_DIM, acc_ref[2 * g], acc_ref[2 * g + 1]).astype(o_ref.dtype)


def _sb(qb16, k16, v16, *, pos0, qb_rows):
    bsz, tq, _ = qb16.shape
    l_pad = k16.shape[1]
    assert tq % qb_rows == 0 and l_pad % LANES == 0
    q_map = lambda b, i: (b, i, 0)
    all_map = lambda b, i: (b, 0, 0)
    resident = dict(pipeline_mode=pl.Buffered(1)) if bsz == 1 else {}
    return pl.pallas_call(
        functools.partial(_sb_kernel, qb_rows=qb_rows, pos0=pos0),
        grid=(bsz, tq // qb_rows),
        in_specs=[pl.BlockSpec((1, qb_rows, D_HEADS), q_map),
                  pl.BlockSpec((1, l_pad, D_HEADS), all_map, **resident),
                  pl.BlockSpec((1, l_pad, D_HEADS), all_map, **resident)],
        out_specs=pl.BlockSpec((1, qb_rows, D_HEADS), q_map),
        out_shape=jax.ShapeDtypeStruct((bsz, tq, D_HEADS), BF16),
        scratch_shapes=[pltpu.VMEM((N_HEADS, qb_rows, LANES), F32),
                        pltpu.VMEM((N_HEADS, qb_rows, LANES), F32)],
        compiler_params=pltpu.CompilerParams(dimension_semantics=("arbitrary", "arbitrary"),
                                             vmem_limit_bytes=VMEM_LIMIT),
        name="sb",
    )(qb16, k16, v16)


def _sbc_kernel(q_ref, kn_ref, vn_ref, kc_hbm, vc_hbm, o_ref, kbuf, vbuf, sem, run_ref, acc_ref,
                *, t_new, p_len):
    b = pl.program_id(0)
    n_ct = p_len // LANES
    qpos = p_len + lax.broadcasted_iota(jnp.int32, (t_new, 1), 0)
    lane1 = lax.broadcasted_iota(jnp.int32, (1, LANES), 1)
    jj = lax.broadcasted_iota(jnp.int32, (LANES, 2 * LANES), 0)
    ss = lax.broadcasted_iota(jnp.int32, (LANES, 2 * LANES), 1)
    sum_rhs = jnp.where((jj > ss) | (ss >= LANES), 1.0, 0.0).astype(BF16)

    def tile_copies(kt, slot):
        rows = pl.ds(kt * LANES, LANES)
        return (pltpu.make_async_copy(kc_hbm.at[b, rows], kbuf.at[slot], sem.at[0, slot]),
                pltpu.make_async_copy(vc_hbm.at[b, rows], vbuf.at[slot], sem.at[1, slot]))

    for cp in tile_copies(n_ct - 1, 0):
        cp.start()

    q = q_ref[0]
    q_heads = [q[:, h * HEAD_DIM:(h + 1) * HEAD_DIM] for h in range(N_HEADS)]
    run_ref[...] = jnp.zeros(run_ref.shape, F32)
    acc_ref[...] = jnp.zeros(acc_ref.shape, F32)

    def tile_update(kt, k_of_head, v_of_head):
        strict = (kt * LANES + lane1) < qpos
        heads = range(N_HEADS)
        z = [_nt_dot(q_heads[h], k_of_head(h)) for h in heads]
        sp = [jnp.maximum(z[h], 0.0) + jnp.log1p(jnp.exp(-jnp.abs(z[h]))) for h in heads]
        ls = [jnp.where(strict, -sp[h], 0.0) for h in heads]
        ls_hi = [ls[h].astype(BF16) for h in heads]
        ls_lo = [(ls[h] - ls_hi[h].astype(F32)).astype(BF16) for h in heads]
        sums = [jnp.dot(ls_hi[h], sum_rhs, preferred_element_type=F32)
                + jnp.dot(ls_lo[h], sum_rhs, preferred_element_type=F32) for h in heads]
        run_old = [run_ref[h] for h in heads]
        wgt = [jnp.where(strict, jnp.exp(z[h] - sp[h] + run_old[h] + sums[h][:, :LANES]), 0.0)
               for h in heads]
        pv = [jnp.dot(wgt[h].astype(BF16), v_of_head(h), preferred_element_type=F32) for h in heads]
        run_new = [run_old[h] + sums[h][:, LANES:] for h in heads]
        alive = run_new[0]
        for h in range(1, N_HEADS):
            alive = jnp.maximum(alive, run_new[h])
        for h in heads:
            acc_ref[h] += pv[h]
            run_ref[h] = run_new[h]
        return jnp.max(alive)

    def pad_keys(x):
        return jnp.concatenate([x, jnp.zeros((LANES - t_new, HEAD_DIM), x.dtype)], axis=0)

    kn = kn_ref[0]
    vn = vn_ref[0]
    top = tile_update(n_ct,
                      lambda h: pad_keys(kn[:, h * HEAD_DIM:(h + 1) * HEAD_DIM]),
                      lambda h: pad_keys(vn[:, h * HEAD_DIM:(h + 1) * HEAD_DIM]))

    def cond(c):
        return c[0] > 0

    def body(c):
        _, kt, slot, _ = c
        for cp in tile_copies(kt, slot):
            cp.wait()

        @pl.when(kt > 0)
        def _prefetch():
            for cp in tile_copies(kt - 1, 1 - slot):
                cp.start()

        top = tile_update(kt,
                          lambda h: kbuf[slot, :, h, :].astype(BF16),
                          lambda h: vbuf[slot, :, h, :].astype(BF16))
        more = kt > 0
        go = jnp.where(more & (top > SB_DEAD), 1, 0)
        return go, kt - 1, 1 - slot, jnp.where(more, 1, 0)

    state = (jnp.where(top > SB_DEAD, 1, 0), jnp.int32(n_ct - 1), jnp.int32(0), jnp.int32(1))
    _, kt_left, slot_left, in_flight = lax.while_loop(cond, body, state)

    @pl.when(in_flight == 1)
    def _drain():
        for cp in tile_copies(kt_left, slot_left):
            cp.wait()

    for h in range(N_HEADS):
        o_ref[0, :, h * HEAD_DIM:(h + 1) * HEAD_DIM] = acc_ref[h].astype(o_ref.dtype)


def _sbc(qb16, k_new, v_new, k_cache, v_cache):
    bsz, t_new, _ = qb16.shape
    p_len = k_cache.shape[1]
    assert p_len % LANES == 0 and p_len >= LANES and t_new % 16 == 0 and t_new <= LANES
    q_map = lambda b: (b, 0, 0)
    return pl.pallas_call(
        functools.partial(_sbc_kernel, t_new=t_new, p_len=p_len),
        grid=(bsz,),
        in_specs=[pl.BlockSpec((1, t_new, D_HEADS), q_map),
                  pl.BlockSpec((1, t_new, D_HEADS), q_map),
                  pl.BlockSpec((1, t_new, D_HEADS), q_map),
                  pl.BlockSpec(memory_space=pl.ANY),
                  pl.BlockSpec(memory_space=pl.ANY)],
        out_specs=pl.BlockSpec((1, t_new, D_HEADS), q_map),
        out_shape=jax.ShapeDtypeStruct((bsz, t_new, D_HEADS), BF16),
        scratch_shapes=[pltpu.VMEM((2, LANES, N_HEADS, HEAD_DIM), F32),
                        pltpu.VMEM((2, LANES, N_HEADS, HEAD_DIM), F32),
                        pltpu.SemaphoreType.DMA((2, 2)),
                        pltpu.VMEM((N_HEADS, t_new, LANES), F32),
                        pltpu.VMEM((N_HEADS, t_new, HEAD_DIM), F32)],
        compiler_params=pltpu.CompilerParams(dimension_semantics=("arbitrary",),
                                             vmem_limit_bytes=VMEM_LIMIT),
        name="sbc",
    )(qb16, k_new, v_new, k_cache, v_cache)


def _post_kernel(oa_ref, ob_ref, sga_ref, sgb_ref, x_ref, wa_ref, wb_ref, wo_ref, g_ref,
                 wr_hi_ref, wr_lo_ref, br_ref, h_o, hn_o, comb_o, *, tm):
    ya = jnp.dot(oa_ref[...], wa_ref[...], preferred_element_type=F32)
    yb = jnp.dot(ob_ref[...], wb_ref[...], preferred_element_type=F32)
    mix = sga_ref[...] * ya + sgb_ref[...] * yb
    h = x_ref[...] + jnp.dot(mix.astype(BF16), wo_ref[...], preferred_element_type=F32)
    h_o[...] = h
    r = lax.rsqrt(jnp.mean(h * h, axis=-1, keepdims=True) + RMS_EPS)
    hn = (h * r) * g_ref[...]
    hn_hi = hn.astype(BF16)
    hn_o[...] = hn_hi
    hn_lo = (hn - hn_hi.astype(F32)).astype(BF16)
    logits = (jnp.dot(hn_hi, wr_hi_ref[...], preferred_element_type=F32)
              + jnp.dot(hn_lo, wr_hi_ref[...], preferred_element_type=F32)
              + jnp.dot(hn_hi, wr_lo_ref[...], preferred_element_type=F32)) + br_ref[...]

    lane = lax.broadcasted_iota(jnp.int32, (tm, LANES), 1)
    is_g = (lane >= N_EXPERTS) & (lane < N_EXPERTS + N_GROUPS)
    gl = jnp.where(is_g, logits, -jnp.inf)
    gmax = jnp.max(gl, axis=1, keepdims=True)
    g_lane = jnp.min(jnp.where(gl == gmax, lane, 2 ** 30), axis=1, keepdims=True)
    g_w = 1.0 / jnp.sum(jnp.exp(gl - gmax), axis=1, keepdims=True)
    in_grp = (lane < N_EXPERTS) & ((lane // EXPERTS_PER_GROUP) == (g_lane - N_EXPERTS))
    e1 = jnp.where(in_grp, logits, -jnp.inf)
    v1 = jnp.max(e1, axis=1, keepdims=True)
    i1 = jnp.min(jnp.where(e1 == v1, lane, 2 ** 30), axis=1, keepdims=True)
    e2 = jnp.where(lane == i1, -jnp.inf, e1)
    v2 = jnp.max(e2, axis=1, keepdims=True)
    i2 = jnp.min(jnp.where(e2 == v2, lane, 2 ** 30), axis=1, keepdims=True)
    t2 = jnp.exp(v2 - v1)
    den = 1.0 + t2
    comb_o[...] = jnp.where(lane == i1, (1.0 / den) * g_w,
                            jnp.where(lane == i2, (t2 / den) * g_w, 0.0))


def _post(oa, ob, sga, sgb, x2d, wa16, wb16, wo16, g_ffn, wr_hi, wr_lo, br):
    n = x2d.shape[0]
    tm = min(256, n)
    assert n % tm == 0
    row = lambda i: (i, 0)
    fix = lambda i: (0, 0)
    return pl.pallas_call(
        functools.partial(_post_kernel, tm=tm),
        grid=(n // tm,),
        in_specs=[pl.BlockSpec((tm, D_HEADS), row), pl.BlockSpec((tm, D_HEADS), row),
                  pl.BlockSpec((tm, D_MODEL), row), pl.BlockSpec((tm, D_MODEL), row),
                  pl.BlockSpec((tm, D_MODEL), row),
                  pl.BlockSpec((D_HEADS, D_MODEL), fix), pl.BlockSpec((D_HEADS, D_MODEL), fix),
                  pl.BlockSpec((D_MODEL, D_MODEL), fix), pl.BlockSpec((1, D_MODEL), fix),
                  pl.BlockSpec((D_MODEL, LANES), fix), pl.BlockSpec((D_MODEL, LANES), fix),
                  pl.BlockSpec((1, LANES), fix)],
        out_specs=[pl.BlockSpec((tm, D_MODEL), row), pl.BlockSpec((tm, D_MODEL), row),
                   pl.BlockSpec((tm, LANES), row)],
        out_shape=[jax.ShapeDtypeStruct((n, D_MODEL), F32),
                   jax.ShapeDtypeStruct((n, D_MODEL), BF16),
                   jax.ShapeDtypeStruct((n, LANES), F32)],
        compiler_params=pltpu.CompilerParams(dimension_semantics=("arbitrary",),
                                             vmem_limit_bytes=VMEM_LIMIT),
        name="post",
    )(oa, ob, sga, sgb, x2d, wa16, wb16, wo16, g_ffn, wr_hi, wr_lo, br)


def _moe_kernel(hn_ref, comb_ref, h_ref, wg_ref, wu_ref, wd_ref, gf_ref, y_o, acc_ref, *, tm):
    e = pl.program_id(1)

    @pl.when(e == 0)
    def _init():
        acc_ref[...] = jnp.zeros(acc_ref.shape, F32)

    hn = hn_ref[...]
    gate = jnp.dot(hn, wg_ref[0].astype(BF16), preferred_element_type=F32)
    up = jnp.dot(hn, wu_ref[0].astype(BF16), preferred_element_type=F32)
    lane = lax.broadcasted_iota(jnp.int32, (tm, LANES), 1)
    c_e = jnp.sum(jnp.where(lane == e, comb_ref[...], 0.0), axis=1, keepdims=True)
    hid = (gate * jax.nn.sigmoid(gate)) * up * c_e
    acc_ref[...] += jnp.dot(hid.astype(BF16), wd_ref[0].astype(BF16), preferred_element_type=F32)

    @pl.when(e == N_EXPERTS - 1)
    def _finish():
        out = h_ref[...] + acc_ref[...]
        r = lax.rsqrt(jnp.mean(out * out, axis=-1, keepdims=True) + RMS_EPS)
        y_o[...] = (out * r) * gf_ref[...]


def _moe(hn16, comb, h, wg16, wu16, wd16, g_final):
    n = h.shape[0]
    tm = min(1024, n)
    assert n % tm == 0
    row = lambda i, e: (i, 0)
    exp = lambda i, e: (e, 0, 0)
    fix = lambda i, e: (0, 0)
    return pl.pallas_call(
        functools.partial(_moe_kernel, tm=tm),
        grid=(n // tm, N_EXPERTS),
        in_specs=[pl.BlockSpec((tm, D_MODEL), row), pl.BlockSpec((tm, LANES), row),
                  pl.BlockSpec((tm, D_MODEL), row),
                  pl.BlockSpec((1, D_MODEL, D_FF_EXPERT), exp),
                  pl.BlockSpec((1, D_MODEL, D_FF_EXPERT), exp),
                  pl.BlockSpec((1, D_FF_EXPERT, D_MODEL), exp),
                  pl.BlockSpec((1, D_MODEL), fix)],
        out_specs=pl.BlockSpec((tm, D_MODEL), row),
        out_shape=jax.ShapeDtypeStruct((n, D_MODEL), F32),
        scratch_shapes=[pltpu.VMEM((tm, D_MODEL), F32)],
        compiler_params=pltpu.CompilerParams(dimension_semantics=("arbitrary", "arbitrary"),
                                             vmem_limit_bytes=VMEM_LIMIT),
        name="moe",
    )(hn16, comb, h, wg16, wu16, wd16, g_final)


def _prep_weights(norm_mix_g, w_in, w_br_a, w_br_b, w_out, norm_ffn_g, w_rg, b_rg, w_re, b_re,
                  w_eg, w_eu, w_ed, norm_final_g):
    offs = np.cumsum(IN_SPLIT)[:-1].tolist()
    q_a, k_a, v_a, q_i, k_i, w_i, q_b, k_b, v_b, g_a, g_b = jnp.split(w_in, offs, axis=-1)
    w_i_pad = jnp.pad(w_i, ((0, 0), (0, LANES - N_IDX_HEADS)))
    w16 = jnp.concatenate([q_a, k_a, v_a, q_i, k_i, k_i, w_i_pad, q_b, k_b, v_b, g_a, g_b],
                          axis=1).astype(BF16)
    half = ROT_HALF
    inv_freq = ROPE_THETA ** (-jnp.arange(half, dtype=F32) / half)
    d = np.arange(LANES) % HEAD_DIM
    invf = jnp.where(jnp.asarray(d < ROT_DIM), inv_freq[jnp.asarray(d % half)], 0.0)[None, :]
    pad = LANES - N_EXPERTS - N_GROUPS
    w_r = jnp.pad(jnp.concatenate([w_re, w_rg], axis=1), ((0, 0), (0, pad)))
    wr_hi = w_r.astype(BF16)
    wr_lo = (w_r - wr_hi.astype(F32)).astype(BF16)
    b_r = jnp.pad(jnp.concatenate([b_re, b_rg]), (0, pad))[None, :]
    return dict(g_mix=norm_mix_g[None, :], w16=w16, invf=invf,
                wa16=w_br_a.astype(BF16), wb16=w_br_b.astype(BF16), wo16=w_out.astype(BF16),
                g_ffn=norm_ffn_g[None, :], wr_hi=wr_hi, wr_lo=wr_lo, b_r=b_r,
                wg=w_eg, wu=w_eu, wd=w_ed,
                g_final=norm_final_g[None, :])


def _layer(x, past, wts, *, dsa_qb, dsa_kt, sb_qb):
    bsz, t, _ = x.shape
    n = bsz * t
    x2d = x.reshape(n, D_MODEL)
    pos0 = 0 if past is None else past[0].shape[1]
    qa_scale = ATT_SCALE * LOG2_E if past is None else ATT_SCALE
    pr = _proj(x2d, wts["g_mix"], wts["w16"], wts["invf"], tq=t, pos0=pos0, qa_scale=qa_scale)
    shp = lambda a: a.reshape(bsz, t, a.shape[-1])

    if past is None:
        vt = pr["va16"].reshape(bsz, t, N_HEADS, HEAD_DIM).transpose(0, 2, 3, 1)
        vt = jnp.concatenate([vt, jnp.ones((bsz, N_HEADS, 1, t), BF16),
                              jnp.zeros((bsz, N_HEADS, LANES - HEAD_DIM - 1, t), BF16)], axis=2)
        oa = _dsat(shp(pr["qa"]), shp(pr["qi"]), shp(pr["wi"]), shp(pr["ki16"]), shp(pr["ka16"]),
                   vt.reshape(bsz, N_HEADS * LANES, t), pos0=0, l_valid=t,
                   qb_rows=dsa_qb, kt_rows=dsa_kt)
        ob = _sb(shp(pr["qb"]), shp(pr["kb16"]), shp(pr["vb16"]), pos0=0, qb_rows=sb_qb)
    else:
        p_len = past[0].shape[1]
        flat16 = lambda c: c.reshape(bsz, p_len, D_HEADS).astype(BF16)
        oa = _dsac(shp(pr["qa"]), shp(pr["qi"]), shp(pr["wi"]), past[2], shp(pr["ki16"]),
                   flat16(past[0]), flat16(past[1]), shp(pr["ka16"]), shp(pr["va16"]),
                   kt_rows=dsa_kt)
        ob = _sbc(shp(pr["qb"]), shp(pr["kb16"]), shp(pr["vb16"]), past[3], past[4])
    h, hn16, comb = _post(oa.reshape(n, D_HEADS), ob.reshape(n, D_HEADS), pr["sga"], pr["sgb"],
                          x2d, wts["wa16"], wts["wb16"], wts["wo16"], wts["g_ffn"],
                          wts["wr_hi"], wts["wr_lo"], wts["b_r"])
    y = _moe(hn16, comb, h, wts["wg"], wts["wu"], wts["wd"], wts["g_final"])
    heads = lambda a: a.reshape(1, bsz, t, N_HEADS, HEAD_DIM)
    rows = (heads(pr["ka"]), heads(pr["va"]),
            pr["ki"][:, :IDX_DIM].reshape(1, bsz, t, IDX_DIM),
            heads(pr["kb"]), heads(pr["vb"]))
    return y.reshape(bsz, t, D_MODEL), rows


def kernel(x_prompt, x_sample, cache_a_k, cache_a_v, cache_idx_k, cache_b_k, cache_b_v,
           norm_mix_g, w_in, w_br_a, w_br_b, w_out, norm_ffn_g,
           w_router_group, b_router_group, w_router_expert, b_router_expert,
           w_exp_gate, w_exp_up, w_exp_down, norm_final_g):
    assert w_in.shape[0] == 1, "single-layer model"
    wts = _prep_weights(norm_mix_g[0], w_in[0], w_br_a[0], w_br_b[0], w_out[0], norm_ffn_g[0],
                        w_router_group[0], b_router_group[0], w_router_expert[0],
                        b_router_expert[0], w_exp_gate[0], w_exp_up[0], w_exp_down[0],
                        norm_final_g)
    y_p, rows_p = _layer(x_prompt, None, wts, dsa_qb=256, dsa_kt=1024, sb_qb=128)
    past = (cache_a_k[0], cache_a_v[0], cache_idx_k[0], cache_b_k[0], cache_b_v[0])
    y_s, rows_s = _layer(x_sample, past, wts, dsa_qb=None, dsa_kt=1024, sb_qb=None)
    return (y_p, y_s) + rows_p + rows_s
```

```python
import functools

import numpy as np
import jax
import jax.numpy as jnp
from jax import lax
from jax.experimental import pallas as pl
from jax.experimental.pallas import tpu as pltpu

D_MODEL = 1024
HEAD_DIM = 64
N_HEADS = 8
D_HEADS = N_HEADS * HEAD_DIM
N_IDX_HEADS = 8
IDX_DIM = 64
IDX_SCALE = (N_IDX_HEADS * IDX_DIM) ** -0.5
ATT_SCALE = HEAD_DIM ** -0.5
LOG2_E = 1.4426950408889634
CHUNK = 64
TOPK_MAX = 256
ROPE_THETA = 500000.0
ROT_DIM = HEAD_DIM // 4
ROT_HALF = ROT_DIM // 2
N_GROUPS = 4
EXPERTS_PER_GROUP = 8
N_EXPERTS = N_GROUPS * EXPERTS_PER_GROUP
D_FF_EXPERT = D_MODEL // 4
RMS_EPS = 1e-6
IN_SPLIT = (D_HEADS, D_HEADS, D_HEADS, N_IDX_HEADS * IDX_DIM, IDX_DIM, N_IDX_HEADS,
            D_HEADS, D_HEADS, D_HEADS, D_MODEL, D_MODEL)

LANES = 128
SUBLANES = 8
N_PAIRS = N_HEADS // 2
NEG_BIG = -1e30
SB_DEAD = -110.0
VMEM_LIMIT = 56 * 1024 * 1024

C_QA, C_KA, C_VA, C_QI, C_KI, C_WI, C_QB, C_KB, C_VB, C_GA, C_GB, C_END = (
    0, 512, 1024, 1536, 2048, 2176, 2304, 2816, 3328, 3840, 4864, 5888)

F32 = jnp.float32
BF16 = jnp.bfloat16


def _nt_dot(a, b):
    return lax.dot_general(a, b, (((1,), (1,)), ((), ())), preferred_element_type=F32)


def _reduce0(x, op, ways=8):
    n = x.shape[0]
    if n % ways != 0 or n <= ways:
        ways = 1
    part = x.reshape(n // ways, ways, *x.shape[1:])
    acc = part[0]
    for j in range(1, n // ways):
        acc = op(acc, part[j])
    out = acc[0]
    for j in range(1, ways):
        out = op(out, acc[j])
    return out


def _head_halves(x):
    lane = lax.broadcasted_iota(jnp.int32, x.shape, 1)
    zero = jnp.zeros_like(x)
    return jnp.where(lane < HEAD_DIM, x, zero), jnp.where(lane >= HEAD_DIM, x, zero)


def _proj_kernel(x_ref, g_ref, w_ref, invf_ref,
                 qa_o, qi_o, qb_o, ka_o, va_o, kb_o, vb_o, ki_o, wi_o,
                 ka16_o, va16_o, kb16_o, vb16_o, ki16_o, sga_o, sgb_o,
                 *, tm, tq, pos0, qa_scale):
    x = x_ref[...]
    r = lax.rsqrt(jnp.mean(x * x, axis=-1, keepdims=True) + RMS_EPS)
    xn = (x * r) * g_ref[...]
    p = jnp.dot(xn.astype(BF16), w_ref[...], preferred_element_type=F32)

    row = pl.program_id(0) * tm + lax.broadcasted_iota(jnp.int32, (tm, 1), 0)
    pos = (row % tq + pos0).astype(F32)
    ang = pos * invf_ref[...]
    c = jnp.cos(ang)
    s = jnp.sin(ang)
    d = lax.broadcasted_iota(jnp.int32, (1, LANES), 1) % HEAD_DIM
    s_lo = jnp.where(d < ROT_HALF, -s, 0.0)
    s_hi = jnp.where((d >= ROT_HALF) & (d < ROT_DIM), s, 0.0)

    def rope(v):
        return (v * c + pltpu.roll(v, LANES - ROT_HALF, 1) * s_lo
                + pltpu.roll(v, ROT_HALF, 1) * s_hi)

    for j in range(D_HEADS // LANES):
        sl = slice(j * LANES, (j + 1) * LANES)
        qa = rope(p[:, C_QA + j * LANES:C_QA + (j + 1) * LANES])
        qa_o[:, sl] = (qa * qa_scale).astype(BF16)
        ka = rope(p[:, C_KA + j * LANES:C_KA + (j + 1) * LANES])
        ka_o[:, sl] = ka
        ka16_o[:, sl] = ka.astype(BF16)
        qi = rope(p[:, C_QI + j * LANES:C_QI + (j + 1) * LANES])
        qi_o[:, sl] = qi.astype(BF16)
    ki = rope(p[:, C_KI:C_KI + LANES])
    ki_o[...] = ki
    ki16_o[...] = ki.astype(BF16)
    wi_o[...] = p[:, C_WI:C_WI + LANES]
    va = p[:, C_VA:C_VA + D_HEADS]
    va_o[...] = va
    va16_o[...] = va.astype(BF16)
    qb_o[...] = (p[:, C_QB:C_QB + D_HEADS] * ATT_SCALE).astype(BF16)
    kb = p[:, C_KB:C_KB + D_HEADS]
    kb_o[...] = kb
    kb16_o[...] = kb.astype(BF16)
    vb = p[:, C_VB:C_VB + D_HEADS]
    vb_o[...] = vb
    vb16_o[...] = vb.astype(BF16)
    sga_o[...] = jax.nn.sigmoid(p[:, C_GA:C_GA + D_MODEL])
    sgb_o[...] = jax.nn.sigmoid(p[:, C_GB:C_GB + D_MODEL])


def _proj(x2d, g, w16, invf, *, tq, pos0, qa_scale):
    n = x2d.shape[0]
    tm = min(256, n)
    assert n % tm == 0
    row = lambda i: (i, 0)
    fix = lambda i: (0, 0)
    widths = dict(qa=D_HEADS, qi=D_HEADS, qb=D_HEADS, ka=D_HEADS, va=D_HEADS, kb=D_HEADS,
                  vb=D_HEADS, ki=LANES, wi=LANES, ka16=D_HEADS, va16=D_HEADS, kb16=D_HEADS,
                  vb16=D_HEADS, ki16=LANES, sga=D_MODEL, sgb=D_MODEL)
    dtypes = dict(qa=BF16, qi=BF16, qb=BF16, ka16=BF16, va16=BF16, kb16=BF16, vb16=BF16, ki16=BF16)
    names = list(widths)
    out_shape = [jax.ShapeDtypeStruct((n, widths[k]), dtypes.get(k, F32)) for k in names]
    out_specs = [pl.BlockSpec((tm, widths[k]), row) for k in names]
    outs = pl.pallas_call(
        functools.partial(_proj_kernel, tm=tm, tq=tq, pos0=pos0, qa_scale=qa_scale),
        grid=(n // tm,),
        in_specs=[pl.BlockSpec((tm, D_MODEL), row),
                  pl.BlockSpec((1, D_MODEL), fix),
                  pl.BlockSpec((D_MODEL, C_END), fix),
                  pl.BlockSpec((1, LANES), fix)],
        out_specs=out_specs,
        out_shape=out_shape,
        compiler_params=pltpu.CompilerParams(dimension_semantics=("arbitrary",),
                                             vmem_limit_bytes=VMEM_LIMIT),
        name="proj",
    )(x2d, g, w16, invf)
    return dict(zip(names, outs))


def _dsac_kernel(qa_ref, qi_ref, wi_ref, kic_ref, kin_ref, kc_ref, vc_ref, kn_ref, vn_ref, o_ref,
                 score_ref, thr_ref, jcut_ref, m_ref, l_ref, acc_ref,
                 *, n_ct, kt_rows, t_new, p_len, topk):
    kt = pl.program_id(1)
    qb_rows = t_new
    ch = kt_rows
    n_ch = n_ct + 1
    l_valid = p_len + t_new

    row = lax.broadcasted_iota(jnp.int32, (qb_rows, 1), 0)
    limit = jnp.minimum(((p_len + row) // CHUNK + 1) * CHUNK, l_valid)

    def pad_keys(x):
        return jnp.concatenate([x, jnp.zeros((LANES - t_new, HEAD_DIM), x.dtype)], axis=0)

    def count_where(pred_fn):
        def body(t, acc):
            blk = score_ref[t]
            for c in range(ch // LANES):
                acc = acc + jnp.where(pred_fn(t, blk[:, c * LANES:(c + 1) * LANES], c), 1.0, 0.0)
            return acc
        acc = lax.fori_loop(0, n_ch, body, jnp.zeros((qb_rows, LANES), F32))
        return jnp.sum(acc, axis=1, keepdims=True)

    @pl.when(kt == 0)
    def _select():
        qi = qi_ref[0]
        q_idx = [qi[:, h * IDX_DIM:(h + 1) * IDX_DIM] for h in range(N_IDX_HEADS)]
        w = wi_ref[0] * IDX_SCALE
        w_cols = [w[:, h:h + 1] for h in range(N_IDX_HEADS)]

        def scores(rel_of_head, n_keys):
            s = jnp.zeros((qb_rows, n_keys), F32)
            for h in range(N_IDX_HEADS):
                s = s + w_cols[h] * jnp.maximum(rel_of_head(h), 0.0)
            return s

        for t in range(n_ct):
            kk_t = kic_ref[0, :, t * ch:(t + 1) * ch].astype(BF16)
            kpos = t * ch + lax.broadcasted_iota(jnp.int32, (1, ch), 1)
            s = scores(lambda h: jnp.dot(q_idx[h], kk_t, preferred_element_type=F32), ch)
            score_ref[t] = jnp.where(kpos < limit, s, -jnp.inf)
        kk_new = pad_keys(kin_ref[0][:, :IDX_DIM])
        s_new = scores(lambda h: _nt_dot(q_idx[h], kk_new), LANES)
        kpos = p_len + lax.broadcasted_iota(jnp.int32, (1, LANES), 1)
        score_ref[n_ct] = jnp.concatenate(
            [jnp.where(kpos < limit, s_new, -jnp.inf),
             jnp.full((qb_rows, ch - LANES), -jnp.inf, F32)], axis=1)

        def minmax_body(t, carry):
            mn, mx = carry
            blk = score_ref[t]
            for c in range(ch // LANES):
                piece = blk[:, c * LANES:(c + 1) * LANES]
                mx = jnp.maximum(mx, piece)
                mn = jnp.minimum(mn, jnp.where(piece == -jnp.inf, jnp.inf, piece))
            return mn, mx
        mn, mx = lax.fori_loop(0, n_ch, minmax_body,
                               (jnp.full((qb_rows, LANES), jnp.inf, F32),
                                jnp.full((qb_rows, LANES), -jnp.inf, F32)))
        lo0 = jnp.min(mn, axis=1, keepdims=True)
        smax = jnp.max(mx, axis=1, keepdims=True)
        hi0 = smax + (jnp.abs(smax) * 1e-6 + 1e-30)
        all_in = limit <= topk
        done0 = jnp.where(all_in, 1.0, 0.0)
        thr0 = jnp.where(all_in, jnp.float32(-3e38), lo0)
        zeros = jnp.zeros((qb_rows, 1), F32)

        def cond(c):
            return c[0] > 0

        def bisect_pass(lo, hi, chi, thr, done, tie, cand, pend):
            half = lo + 0.5 * (hi - lo)
            has_cand = pend > 0.5
            mid = jnp.where(has_cand, cand, half)
            stuck = jnp.logical_not(has_cand) & ((half <= lo) | (half >= hi))
            mid_b = jnp.broadcast_to(mid, (qb_rows, LANES))
            cnt = count_where(lambda t, piece, cidx: piece >= mid_b)
            active = done < 0.5
            moving = active & jnp.logical_not(stuck)
            found = moving & ((cnt == topk) | (has_cand & (cnt > topk)))
            now_tie = (active & stuck) | (moving & has_cand & (cnt > topk))
            up = moving & jnp.logical_not(has_cand) & (cnt > topk)
            down = moving & (cnt < topk)
            thr = jnp.where(found, mid, jnp.where(active & stuck, lo, thr))
            tie = jnp.where(now_tie, 1.0, tie)
            done = jnp.where(found | now_tie, 1.0, done)
            lo = jnp.where(up, mid, lo)
            chi = jnp.where(down, cnt, chi)
            hi = jnp.where(down, mid, hi)
            return lo, hi, chi, thr, done, tie, cand, jnp.zeros_like(pend)

        def snap_pass(lo, hi, chi, thr, done, tie, cand, pend):
            hi_b = jnp.broadcast_to(hi, (qb_rows, LANES))

            def body(t, acc):
                blk = score_ref[t]
                for c in range(ch // LANES):
                    piece = blk[:, c * LANES:(c + 1) * LANES]
                    acc = jnp.maximum(acc, jnp.where(piece < hi_b, piece, -jnp.inf))
                return acc
            acc = lax.fori_loop(0, n_ch, body, jnp.full((qb_rows, LANES), -jnp.inf, F32))
            cand = jnp.max(acc, axis=1, keepdims=True)
            return lo, hi, chi, thr, done, tie, cand, 1.0 - done

        def body(c):
            it = c[1]
            snap = (it >= 11) & (it % 4 == 3)
            new = lax.cond(snap, snap_pass, bisect_pass, *c[2:])
            left = jnp.sum(1.0 - new[4])
            go = jnp.where((left > 0) & (it < 400), 1, 0)
            return (go, it + 1) + tuple(new)

        left0 = jnp.sum(1.0 - done0)
        state = (jnp.where(left0 > 0, 1, 0), jnp.int32(0), lo0, hi0, zeros, thr0, done0, zeros,
                 zeros, zeros)
        _, _, _, _, chi, thr, _, tie, _, _ = lax.while_loop(cond, body, state)

        big = jnp.full((qb_rows, 1), 2 ** 30, jnp.int32)

        def tie_break():
            need = topk - chi
            thr_b = jnp.broadcast_to(thr, (qb_rows, LANES))
            lane1 = lax.broadcasted_iota(jnp.int32, (1, LANES), 1)

            def tb_body(_, c):
                lo_j, hi_j = c
                mid_j = lo_j + (hi_j - lo_j) // 2
                mid_jb = jnp.broadcast_to(mid_j, (qb_rows, LANES))
                cnt = count_where(lambda t, piece, cidx: (piece == thr_b)
                                  & (t * ch + cidx * LANES + lane1 <= mid_jb))
                ok = cnt >= need
                return jnp.where(ok, lo_j, mid_j), jnp.where(ok, mid_j, hi_j)

            lo_j = jnp.full((qb_rows, 1), -1, jnp.int32)
            hi_j = jnp.zeros((qb_rows, 1), jnp.int32) + (n_ch * ch - 1)
            n_steps = int(np.ceil(np.log2(score_ref.shape[0] * ch))) + 1
            _, hi_j = lax.fori_loop(0, n_steps, tb_body, (lo_j, hi_j))
            return jnp.where(tie > 0, hi_j, big)

        jcut = lax.cond(jnp.sum(tie) > 0, tie_break, lambda: big)
        thr_ref[...] = jnp.broadcast_to(thr, (qb_rows, LANES))
        jcut_ref[...] = jnp.broadcast_to(jcut, (qb_rows, LANES))
        m_ref[...] = jnp.full(m_ref.shape, NEG_BIG, F32)
        l_ref[...] = jnp.zeros(l_ref.shape, F32)
        acc_ref[...] = jnp.zeros(acc_ref.shape, F32)

    def update(n_keys, logits_of_head, pv_of_head):
        thr_b = thr_ref[...]
        jcut_b = jcut_ref[...]
        lane1 = lax.broadcasted_iota(jnp.int32, (1, LANES), 1)
        blk = score_ref[kt]
        bias = []
        for c in range(n_keys // LANES):
            piece = blk[:, c * LANES:(c + 1) * LANES]
            kpos = kt * kt_rows + c * LANES + lane1
            sel = (piece >= thr_b) & ((piece > thr_b) | (kpos <= jcut_b))
            bias.append(jnp.where(sel, 0.0, NEG_BIG))
        n_piece = len(bias)
        qa = qa_ref[0]
        for h in range(N_HEADS):
            logits = logits_of_head(h, qa[:, h * HEAD_DIM:(h + 1) * HEAD_DIM])
            pieces = [logits[:, c * LANES:(c + 1) * LANES] + bias[c] for c in range(n_piece)]
            mx = pieces[0]
            for c in range(1, n_piece):
                mx = jnp.maximum(mx, pieces[c])
            m_old = m_ref[h]
            m_new = jnp.maximum(m_old, jnp.max(mx, axis=1, keepdims=True))
            alpha = jnp.exp(m_old - m_new)
            ps = [jnp.exp(pc - m_new) for pc in pieces]
            psum = ps[0]
            for c in range(1, n_piece):
                psum = psum + ps[c]
            p16 = jnp.concatenate([pc.astype(BF16) for pc in ps], axis=1)
            m_ref[h] = m_new
            l_ref[h] = alpha * l_ref[h] + psum
            acc_ref[h] = alpha[:, :HEAD_DIM] * acc_ref[h] + pv_of_head(h, p16)

    @pl.when(kt < n_ct)
    def _cache_tile():
        update(kt_rows,
               lambda h, q_h: jnp.dot(q_h, kc_ref[0, h].astype(BF16), preferred_element_type=F32),
               lambda h, p16: _nt_dot(p16, vc_ref[0, h].astype(BF16)))

    @pl.when(kt == n_ct)
    def _new_tile():
        kn = kn_ref[0]
        vn = vn_ref[0]
        head = lambda x, h: pad_keys(x[:, h * HEAD_DIM:(h + 1) * HEAD_DIM])
        update(LANES,
               lambda h, q_h: _nt_dot(q_h, head(kn, h)),
               lambda h, p16: jnp.dot(p16, head(vn, h), preferred_element_type=F32))
        for h in range(N_HEADS):
            denom = jnp.sum(l_ref[h], axis=1, keepdims=True)
            o_ref[0, :, h * HEAD_DIM:(h + 1) * HEAD_DIM] = (acc_ref[h] / denom).astype(o_ref.dtype)


def _dsat_kernel(qb_of, kt_of, first_of, last_of,
                 qa_ref, qi_ref, wi_ref, ki_ref, k_ref, vt_ref, o_ref,
                 score_ref, m_ref, acc_ref,
                 *, qb_rows, kt_rows, pos0, l_valid, topk):
    i = pl.program_id(1)
    qb = qb_of[i]
    kt = kt_of[i]
    nq = qb_rows
    ns = kt_rows // SUBLANES

    qcol = qb * nq + lax.broadcasted_iota(jnp.int32, (SUBLANES, nq), 1)
    limit = jnp.minimum(((pos0 + qcol) // CHUNK + 1) * CHUNK, l_valid)
    max_limit = jnp.minimum(((pos0 + qb * nq + nq - 1) // CHUNK + 1) * CHUNK, l_valid)
    n_kt = (max_limit + kt_rows - 1) // kt_rows
    kidx = (lax.broadcasted_iota(jnp.int32, (ns, SUBLANES, nq), 0) * SUBLANES
            + lax.broadcasted_iota(jnp.int32, (ns, SUBLANES, nq), 1))

    def reduce_tiles(fn, op, init):
        def body(t, acc):
            return op(acc, fn(t, score_ref[t]))
        return lax.fori_loop(0, n_kt, body, init)

    def count_where(pred):
        acc = reduce_tiles(lambda t, blk: _reduce0(jnp.where(pred(t, blk), 1.0, 0.0), jnp.add),
                           jnp.add, jnp.zeros((SUBLANES, nq), F32))
        return jnp.broadcast_to(jnp.sum(acc, axis=0, keepdims=True), (SUBLANES, nq))

    @pl.when(first_of[i] == 1)
    def _select():
        qi = qi_ref[0]
        q_halves = []
        for g in range(N_PAIRS):
            q_halves.extend(_head_halves(qi[:, g * LANES:(g + 1) * LANES]))
        w_t = wi_ref[0].T * IDX_SCALE
        w_rows = [jnp.broadcast_to(w_t[h:h + 1, :], (SUBLANES, nq)) for h in range(N_IDX_HEADS)]

        def score_body(t, carry):
            mn, mx, n_pos, n_nonneg = carry
            kk = ki_ref[0, pl.ds(pl.multiple_of(t * kt_rows, kt_rows), kt_rows), :]
            s = jnp.zeros((ns, SUBLANES, nq), F32)
            for h in range(N_IDX_HEADS):
                rel = jnp.maximum(_nt_dot(kk, q_halves[h]), 0.0)
                s = s + w_rows[h] * rel.reshape(ns, SUBLANES, nq)
            seen = kidx + t * kt_rows < limit
            s_lo = jnp.where(seen, s, -jnp.inf)
            score_ref[t] = s_lo
            mn = jnp.minimum(mn, _reduce0(jnp.where(seen, s, jnp.inf), jnp.minimum))
            mx = jnp.maximum(mx, _reduce0(s_lo, jnp.maximum))
            n_pos = n_pos + _reduce0(jnp.where(s_lo > 0.0, 1.0, 0.0), jnp.add)
            n_nonneg = n_nonneg + _reduce0(jnp.where(s_lo >= 0.0, 1.0, 0.0), jnp.add)
            return mn, mx, n_pos, n_nonneg
        zeros = jnp.zeros((SUBLANES, nq), F32)
        mn8, mx8, n_pos8, n_nonneg8 = lax.fori_loop(
            0, n_kt, score_body,
            (jnp.full((SUBLANES, nq), jnp.inf, F32), jnp.full((SUBLANES, nq), -jnp.inf, F32),
             zeros, zeros))

        rep = lambda v: jnp.broadcast_to(v, (SUBLANES, nq))
        lo0 = rep(jnp.min(mn8, axis=0, keepdims=True))
        smax = rep(jnp.max(mx8, axis=0, keepdims=True))
        hi0 = smax + (jnp.abs(smax) * 1e-6 + 1e-30)
        n_pos = rep(jnp.sum(n_pos8, axis=0, keepdims=True))
        n_nonneg = rep(jnp.sum(n_nonneg8, axis=0, keepdims=True))
        all_in = limit <= topk
        below0 = jnp.logical_not(all_in) & (n_nonneg < topk)
        at0 = jnp.logical_not(all_in) & (n_pos < topk) & (n_nonneg >= topk)
        above0 = jnp.logical_not(all_in) & (n_pos >= topk)
        hi0 = jnp.where(below0, 0.0, hi0)
        chi0 = jnp.where(below0, n_nonneg, jnp.where(at0, n_pos, 0.0))
        lo0 = jnp.where(above0, 0.0, lo0)
        done0 = jnp.where(all_in | at0, 1.0, 0.0)
        tie0 = jnp.where(at0 & (n_nonneg > topk), 1.0, 0.0)
        thr0 = jnp.where(all_in, jnp.float32(-3e38), jnp.where(at0, 0.0, lo0))

        def bisect_pass(lo, hi, chi, thr, done, tie, cand, pend):
            half = lo + 0.5 * (hi - lo)
            has_cand = pend > 0.5
            mid = jnp.where(has_cand, cand, half)
            stuck = jnp.logical_not(has_cand) & ((half <= lo) | (half >= hi))
            cnt = count_where(lambda t, blk: blk >= mid)
            active = done < 0.5
            moving = active & jnp.logical_not(stuck)
            found = moving & ((cnt == topk) | (has_cand & (cnt > topk)))
            now_tie = (active & stuck) | (moving & has_cand & (cnt > topk))
            up = moving & jnp.logical_not(has_cand) & (cnt > topk)
            down = moving & (cnt < topk)
            thr = jnp.where(found, mid, jnp.where(active & stuck, lo, thr))
            tie = jnp.where(now_tie, 1.0, tie)
            done = jnp.where(found | now_tie, 1.0, done)
            lo = jnp.where(up, mid, lo)
            chi = jnp.where(down, cnt, chi)
            hi = jnp.where(down, mid, hi)
            return lo, hi, chi, thr, done, tie, cand, jnp.zeros_like(pend)

        def snap_pass(lo, hi, chi, thr, done, tie, cand, pend):
            below = reduce_tiles(
                lambda t, blk: _reduce0(jnp.where(blk < hi, blk, -jnp.inf), jnp.maximum),
                jnp.maximum, jnp.full((SUBLANES, nq), -jnp.inf, F32))
            cand = rep(jnp.max(below, axis=0, keepdims=True))
            return lo, hi, chi, thr, done, tie, cand, 1.0 - done

        def cond(c):
            return c[0] > 0

        def body(c):
            it = c[1]
            snap = (it >= 11) & (it % 4 == 3)
            new = lax.cond(snap, snap_pass, bisect_pass, *c[2:])
            left = jnp.sum(1.0 - new[4])
            go = jnp.where((left > 0) & (it < 400), 1, 0)
            return (go, it + 1) + tuple(new)

        left0 = jnp.sum(1.0 - done0)
        state = (jnp.where(left0 > 0, 1, 0), jnp.int32(0), lo0, hi0, chi0, thr0, done0, tie0,
                 zeros, zeros)
        _, _, _, _, chi, thr, _, tie, _, _ = lax.while_loop(cond, body, state)

        big = jnp.full((SUBLANES, nq), 2 ** 30, jnp.int32)

        def tie_break():
            need = topk - chi

            def tb_body(_, c):
                lo_j, hi_j = c
                mid_j = lo_j + (hi_j - lo_j) // 2
                cnt = count_where(lambda t, blk: (blk == thr) & (kidx + t * kt_rows <= mid_j))
                ok = cnt >= need
                return jnp.where(ok, lo_j, mid_j), jnp.where(ok, mid_j, hi_j)

            lo_j = jnp.full((SUBLANES, nq), -1, jnp.int32)
            hi_j = jnp.zeros((SUBLANES, nq), jnp.int32) + (n_kt * kt_rows - 1)
            n_steps = int(np.ceil(np.log2(score_ref.shape[0] * kt_rows))) + 1
            _, hi_j = lax.fori_loop(0, n_steps, tb_body, (lo_j, hi_j))
            return jnp.where(tie > 0, hi_j, big)

        jcut = lax.cond(jnp.sum(tie) > 0, tie_break, lambda: big)

        def mask_body(t, carry):
            blk = score_ref[t]
            sel = (blk >= thr) & ((blk > thr) | (kidx + t * kt_rows <= jcut))
            score_ref[t] = jnp.where(sel, 0.0, NEG_BIG)
            return carry
        lax.fori_loop(0, n_kt, mask_body, 0)

        m_ref[...] = jnp.full(m_ref.shape, NEG_BIG, F32)
        acc_ref[...] = jnp.zeros(acc_ref.shape, F32)

    bias = score_ref[kt]
    qa = qa_ref[0]
    q_heads = []
    for g in range(N_PAIRS):
        q_heads.extend(_head_halves(qa[:, g * LANES:(g + 1) * LANES]))
    for h0 in range(0, N_HEADS, 4):
        heads = range(h0, h0 + 4)
        logits = {h: _nt_dot(k_ref[0, :, (h // 2) * LANES:(h // 2 + 1) * LANES], q_heads[h])
                  .reshape(ns, SUBLANES, nq) + bias for h in heads}
        m_old = {h: m_ref[h] for h in heads}
        m_new = {h: jnp.maximum(m_old[h], jnp.max(_reduce0(logits[h], jnp.maximum),
                                                  axis=0, keepdims=True)) for h in heads}
        alpha = {h: jnp.exp2(m_old[h] - m_new[h]) for h in heads}
        p = {h: jnp.exp2(logits[h] - m_new[h]) for h in heads}
        pv = {h: jnp.dot(vt_ref[0, h * LANES:(h + 1) * LANES, :],
                         p[h].reshape(kt_rows, nq).astype(BF16), preferred_element_type=F32)
              for h in heads}
        for h in heads:
            acc_ref[h] = alpha[h][0:1] * acc_ref[h] + pv[h]
            m_ref[h] = m_new[h]

    @pl.when(last_of[i] == 1)
    def _finish():
        for g in range(N_PAIRS):
            outs = []
            for h in (2 * g, 2 * g + 1):
                acc = acc_ref[h]
                outs.append(acc[:HEAD_DIM] / acc[HEAD_DIM:HEAD_DIM + 1])
            o_t = jnp.concatenate(outs, axis=0)
            o_ref[0, :, g * LANES:(g + 1) * LANES] = o_t.T.astype(o_ref.dtype)


def _dsat(qa, qi, wi, ki16, k16, vt16, *, pos0, l_valid, qb_rows, kt_rows):
    bsz, tq, _ = qa.shape
    l_pad = k16.shape[1]
    assert tq % qb_rows == 0 and l_pad % kt_rows == 0 and qb_rows % LANES == 0
    topk = min(TOPK_MAX, l_valid // 4)
    tables = _dsa_tables(tq, qb_rows, kt_rows, pos0, l_valid)
    q_map = lambda b, i, qb_of, kt_of, f, l: (b, qb_of[i], 0)
    k_map = lambda b, i, qb_of, kt_of, f, l: (b, kt_of[i], 0)
    vt_map = lambda b, i, qb_of, kt_of, f, l: (b, 0, kt_of[i])
    all_map = lambda b, i, qb_of, kt_of, f, l: (b, 0, 0)
    grid_spec = pltpu.PrefetchScalarGridSpec(
        num_scalar_prefetch=4,
        grid=(bsz, tables[0].shape[0]),
        in_specs=[pl.BlockSpec((1, qb_rows, D_HEADS), q_map),
                  pl.BlockSpec((1, qb_rows, D_HEADS), q_map),
                  pl.BlockSpec((1, qb_rows, LANES), q_map),
                  pl.BlockSpec((1, l_pad, LANES), all_map),
                  pl.BlockSpec((1, kt_rows, D_HEADS), k_map),
                  pl.BlockSpec((1, N_HEADS * LANES, kt_rows), vt_map)],
        out_specs=pl.BlockSpec((1, qb_rows, D_HEADS), q_map),
        scratch_shapes=[pltpu.VMEM((l_pad // kt_rows, kt_rows // SUBLANES, SUBLANES, qb_rows), F32),
                        pltpu.VMEM((N_HEADS, SUBLANES, qb_rows), F32),
                        pltpu.VMEM((N_HEADS, LANES, qb_rows), F32)])
    return pl.pallas_call(
        functools.partial(_dsat_kernel, qb_rows=qb_rows, kt_rows=kt_rows, pos0=pos0,
                          l_valid=l_valid, topk=topk),
        grid_spec=grid_spec,
        out_shape=jax.ShapeDtypeStruct((bsz, tq, D_HEADS), BF16),
        compiler_params=pltpu.CompilerParams(dimension_semantics=("arbitrary", "arbitrary"),
                                             vmem_limit_bytes=VMEM_LIMIT),
        name="dsat",
    )(*tables, qa, qi, wi, ki16, k16, vt16)


def _dsa_tables(tq, qb_rows, kt_rows, pos0, l_valid):
    qb_l, kt_l, first_l, last_l = [], [], [], []
    for qb in range(tq // qb_rows):
        max_limit = min(((pos0 + qb * qb_rows + qb_rows - 1) // CHUNK + 1) * CHUNK, l_valid)
        n_kt = -(-max_limit // kt_rows)
        for kt in range(n_kt):
            qb_l.append(qb)
            kt_l.append(kt)
            first_l.append(int(kt == 0))
            last_l.append(int(kt == n_kt - 1))
    return [jnp.asarray(np.asarray(a, np.int32)) for a in (qb_l, kt_l, first_l, last_l)]


def _dsac(qa, qi, wi, ki_cache, ki_new, k_cache, v_cache, k_new, v_new, *, kt_rows):
    bsz, t_new, _ = qa.shape
    p_len = k_cache.shape[3]
    assert p_len % kt_rows == 0 and kt_rows % LANES == 0 and kt_rows > LANES
    assert t_new % 16 == 0 and t_new <= LANES
    n_ct = p_len // kt_rows
    topk = min(TOPK_MAX, (p_len + t_new) // 4)
    q_map = lambda b, j: (b, 0, 0)
    c_map = lambda b, j: (b, 0, 0, jnp.minimum(j, n_ct - 1))
    return pl.pallas_call(
        functools.partial(_dsac_kernel, n_ct=n_ct, kt_rows=kt_rows, t_new=t_new, p_len=p_len,
                          topk=topk),
        grid=(bsz, n_ct + 1),
        in_specs=[pl.BlockSpec((1, t_new, D_HEADS), q_map),
                  pl.BlockSpec((1, t_new, D_HEADS), q_map),
                  pl.BlockSpec((1, t_new, LANES), q_map),
                  pl.BlockSpec((1, IDX_DIM, p_len), q_map),
                  pl.BlockSpec((1, t_new, LANES), q_map),
                  pl.BlockSpec((1, N_HEADS, HEAD_DIM, kt_rows), c_map),
                  pl.BlockSpec((1, N_HEADS, HEAD_DIM, kt_rows), c_map),
                  pl.BlockSpec((1, t_new, D_HEADS), q_map),
                  pl.BlockSpec((1, t_new, D_HEADS), q_map)],
        out_specs=pl.BlockSpec((1, t_new, D_HEADS), q_map),
        out_shape=jax.ShapeDtypeStruct((bsz, t_new, D_HEADS), BF16),
        scratch_shapes=[pltpu.VMEM((n_ct + 1, t_new, kt_rows), F32),
                        pltpu.VMEM((t_new, LANES), F32),
                        pltpu.VMEM((t_new, LANES), jnp.int32),
                        pltpu.VMEM((N_HEADS, t_new, LANES), F32),
                        pltpu.VMEM((N_HEADS, t_new, LANES), F32),
                        pltpu.VMEM((N_HEADS, t_new, HEAD_DIM), F32)],
        compiler_params=pltpu.CompilerParams(dimension_semantics=("arbitrary", "arbitrary"),
                                             vmem_limit_bytes=VMEM_LIMIT),
        name="dsac",
    )(qa, qi, wi, ki_cache, ki_new, k_cache, v_cache, k_new, v_new)


def _sb_kernel(q_ref, k_ref, v_ref, o_ref, run_ref, acc_ref, *, qb_rows, pos0):
    qb = pl.program_id(1)
    qpos = pos0 + qb * qb_rows + lax.broadcasted_iota(jnp.int32, (qb_rows, 1), 0)
    kt_top = (pos0 + qb * qb_rows + qb_rows - 1) // LANES
    lane1 = lax.broadcasted_iota(jnp.int32, (1, LANES), 1)
    jj = lax.broadcasted_iota(jnp.int32, (LANES, 2 * LANES), 0)
    ss = lax.broadcasted_iota(jnp.int32, (LANES, 2 * LANES), 1)
    sum_rhs = jnp.where((jj > ss) | (ss >= LANES), 1.0, 0.0).astype(BF16)
    lane = lax.broadcasted_iota(jnp.int32, (qb_rows, LANES), 1)

    q = q_ref[0]
    q_heads = []
    for g in range(N_PAIRS):
        q_heads.extend(_head_halves(q[:, g * LANES:(g + 1) * LANES]))
    run_ref[...] = jnp.zeros(run_ref.shape, F32)
    acc_ref[...] = jnp.zeros(acc_ref.shape, F32)

    def cond(c):
        return c[0] > 0

    def body(c):
        _, kt = c
        rows = pl.ds(pl.multiple_of(kt * LANES, LANES), LANES)
        strict = (kt * LANES + lane1) < qpos
        heads = range(N_HEADS)
        k_t = [k_ref[0, rows, g * LANES:(g + 1) * LANES] for g in range(N_PAIRS)]
        v_t = [v_ref[0, rows, g * LANES:(g + 1) * LANES] for g in range(N_PAIRS)]
        z = [_nt_dot(q_heads[h], k_t[h // 2]) for h in heads]
        sp = [jnp.maximum(z[h], 0.0) + jnp.log1p(jnp.exp(-jnp.abs(z[h]))) for h in heads]
        ls = [jnp.where(strict, -sp[h], 0.0) for h in heads]
        ls_hi = [ls[h].astype(BF16) for h in heads]
        ls_lo = [(ls[h] - ls_hi[h].astype(F32)).astype(BF16) for h in heads]
        sums = [jnp.dot(ls_hi[h], sum_rhs, preferred_element_type=F32)
                + jnp.dot(ls_lo[h], sum_rhs, preferred_element_type=F32) for h in heads]
        run_old = [run_ref[h] for h in heads]
        wgt = [jnp.where(strict, jnp.exp(z[h] - sp[h] + run_old[h] + sums[h][:, :LANES]), 0.0)
               for h in heads]
        pv = [jnp.dot(wgt[h].astype(BF16), v_t[h // 2], preferred_element_type=F32) for h in heads]
        run_new = [run_old[h] + sums[h][:, LANES:] for h in heads]
        alive = run_new[0]
        for h in range(1, N_HEADS):
            alive = jnp.maximum(alive, run_new[h])
        for h in heads:
            acc_ref[h] += pv[h]
            run_ref[h] = run_new[h]
        go = jnp.where((kt > 0) & (jnp.max(alive) > SB_DEAD), 1, 0)
        return go, kt - 1

    lax.while_loop(cond, body, (jnp.int32(1), kt_top))
    for g in range(N_PAIRS):
        o_ref[0, :, g * LANES:(g + 1) * LANES] = jnp.where(
            lane < HEAD_DIM, acc_ref[2 * g], acc_ref[2 * g + 1]).astype(o_ref.dtype)


def _sb(qb16, k16, v16, *, pos0, qb_rows):
    bsz, tq, _ = qb16.shape
    l_pad = k16.shape[1]
    assert tq % qb_rows == 0 and l_pad % LANES == 0
    q_map = lambda b, i: (b, i, 0)
    all_map = lambda b, i: (b, 0, 0)
    resident = dict(pipeline_mode=pl.Buffered(1)) if bsz == 1 else {}
    return pl.pallas_call(
        functools.partial(_sb_kernel, qb_rows=qb_rows, pos0=pos0),
        grid=(bsz, tq // qb_rows),
        in_specs=[pl.BlockSpec((1, qb_rows, D_HEADS), q_map),
                  pl.BlockSpec((1, l_pad, D_HEADS), all_map, **resident),
                  pl.BlockSpec((1, l_pad, D_HEADS), all_map, **resident)],
        out_specs=pl.BlockSpec((1, qb_rows, D_HEADS), q_map),
        out_shape=jax.ShapeDtypeStruct((bsz, tq, D_HEADS), BF16),
        scratch_shapes=[pltpu.VMEM((N_HEADS, qb_rows, LANES), F32),
                        pltpu.VMEM((N_HEADS, qb_rows, LANES), F32)],
        compiler_params=pltpu.CompilerParams(dimension_semantics=("arbitrary", "arbitrary"),
                                             vmem_limit_bytes=VMEM_LIMIT),
        name="sb",
    )(qb16, k16, v16)


def _sbc_kernel(q_ref, kn_ref, vn_ref, kc_hbm, vc_hbm, o_ref, kbuf, vbuf, sem, run_ref, acc_ref,
                *, t_new, p_len):
    b = pl.program_id(0)
    n_ct = p_len // LANES
    qpos = p_len + lax.broadcasted_iota(jnp.int32, (t_new, 1), 0)
    lane1 = lax.broadcasted_iota(jnp.int32, (1, LANES), 1)
    jj = lax.broadcasted_iota(jnp.int32, (LANES, 2 * LANES), 0)
    ss = lax.broadcasted_iota(jnp.int32, (LANES, 2 * LANES), 1)
    sum_rhs = jnp.where((jj > ss) | (ss >= LANES), 1.0, 0.0).astype(BF16)

    def tile_copies(kt, slot):
        keys = pl.ds(pl.multiple_of(kt * LANES, LANES), LANES)
        return (pltpu.make_async_copy(kc_hbm.at[b, :, :, keys], kbuf.at[slot], sem.at[0, slot]),
                pltpu.make_async_copy(vc_hbm.at[b, :, :, keys], vbuf.at[slot], sem.at[1, slot]))

    for cp in tile_copies(n_ct - 1, 0):
        cp.start()

    q = q_ref[0]
    q_heads = [q[:, h * HEAD_DIM:(h + 1) * HEAD_DIM] for h in range(N_HEADS)]
    run_ref[...] = jnp.zeros(run_ref.shape, F32)
    acc_ref[...] = jnp.zeros(acc_ref.shape, F32)

    def tile_update(kt, z_of_head, pv_of_head):
        strict = (kt * LANES + lane1) < qpos
        heads = range(N_HEADS)
        z = [z_of_head(h) for h in heads]
        sp = [jnp.maximum(z[h], 0.0) + jnp.log1p(jnp.exp(-jnp.abs(z[h]))) for h in heads]
        ls = [jnp.where(strict, -sp[h], 0.0) for h in heads]
        ls_hi = [ls[h].astype(BF16) for h in heads]
        ls_lo = [(ls[h] - ls_hi[h].astype(F32)).astype(BF16) for h in heads]
        sums = [jnp.dot(ls_hi[h], sum_rhs, preferred_element_type=F32)
                + jnp.dot(ls_lo[h], sum_rhs, preferred_element_type=F32) for h in heads]
        run_old = [run_ref[h] for h in heads]
        wgt = [jnp.where(strict, jnp.exp(z[h] - sp[h] + run_old[h] + sums[h][:, :LANES]), 0.0)
               for h in heads]
        pv = [pv_of_head(h, wgt[h].astype(BF16)) for h in heads]
        run_new = [run_old[h] + sums[h][:, LANES:] for h in heads]
        alive = run_new[0]
        for h in range(1, N_HEADS):
            alive = jnp.maximum(alive, run_new[h])
        for h in heads:
            acc_ref[h] += pv[h]
            run_ref[h] = run_new[h]
        return jnp.max(alive)

    def new_head(x, h):
        return jnp.concatenate([x[:, h * HEAD_DIM:(h + 1) * HEAD_DIM],
                                jnp.zeros((LANES - t_new, HEAD_DIM), x.dtype)], axis=0)

    kn = kn_ref[0]
    vn = vn_ref[0]
    top = tile_update(n_ct,
                      lambda h: _nt_dot(q_heads[h], new_head(kn, h)),
                      lambda h, w16: jnp.dot(w16, new_head(vn, h), preferred_element_type=F32))

    def cond(c):
        return c[0] > 0

    def body(c):
        _, kt, slot, _ = c
        for cp in tile_copies(kt, slot):
            cp.wait()

        @pl.when(kt > 0)
        def _prefetch():
            for cp in tile_copies(kt - 1, 1 - slot):
                cp.start()

        top = tile_update(
            kt,
            lambda h: jnp.dot(q_heads[h], kbuf[slot, h].astype(BF16), preferred_element_type=F32),
            lambda h, w16: _nt_dot(w16, vbuf[slot, h].astype(BF16)))
        more = kt > 0
        go = jnp.where(more & (top > SB_DEAD), 1, 0)
        return go, kt - 1, 1 - slot, jnp.where(more, 1, 0)

    state = (jnp.where(top > SB_DEAD, 1, 0), jnp.int32(n_ct - 1), jnp.int32(0), jnp.int32(1))
    _, kt_left, slot_left, in_flight = lax.while_loop(cond, body, state)

    @pl.when(in_flight == 1)
    def _drain():
        for cp in tile_copies(kt_left, slot_left):
            cp.wait()

    for h in range(N_HEADS):
        o_ref[0, :, h * HEAD_DIM:(h + 1) * HEAD_DIM] = acc_ref[h].astype(o_ref.dtype)


def _sbc(qb16, k_new, v_new, k_cache, v_cache):
    bsz, t_new, _ = qb16.shape
    p_len = k_cache.shape[3]
    assert p_len % LANES == 0 and p_len >= LANES and t_new % 16 == 0 and t_new <= LANES
    q_map = lambda b: (b, 0, 0)
    return pl.pallas_call(
        functools.partial(_sbc_kernel, t_new=t_new, p_len=p_len),
        grid=(bsz,),
        in_specs=[pl.BlockSpec((1, t_new, D_HEADS), q_map),
                  pl.BlockSpec((1, t_new, D_HEADS), q_map),
                  pl.BlockSpec((1, t_new, D_HEADS), q_map),
                  pl.BlockSpec(memory_space=pl.ANY),
                  pl.BlockSpec(memory_space=pl.ANY)],
        out_specs=pl.BlockSpec((1, t_new, D_HEADS), q_map),
        out_shape=jax.ShapeDtypeStruct((bsz, t_new, D_HEADS), BF16),
        scratch_shapes=[pltpu.VMEM((2, N_HEADS, HEAD_DIM, LANES), F32),
                        pltpu.VMEM((2, N_HEADS, HEAD_DIM, LANES), F32),
                        pltpu.SemaphoreType.DMA((2, 2)),
                        pltpu.VMEM((N_HEADS, t_new, LANES), F32),
                        pltpu.VMEM((N_HEADS, t_new, HEAD_DIM), F32)],
        compiler_params=pltpu.CompilerParams(dimension_semantics=("arbitrary",),
                                             vmem_limit_bytes=VMEM_LIMIT),
        name="sbc",
    )(qb16, k_new, v_new, k_cache, v_cache)


def _post_kernel(oa_ref, ob_ref, sga_ref, sgb_ref, x_ref, wa_ref, wb_ref, wo_ref, g_ref,
                 wr_hi_ref, wr_lo_ref, br_ref, h_o, hn_o, comb_o, *, tm):
    ya = jnp.dot(oa_ref[...], wa_ref[...], preferred_element_type=F32)
    yb = jnp.dot(ob_ref[...], wb_ref[...], preferred_element_type=F32)
    mix = sga_ref[...] * ya + sgb_ref[...] * yb
    h = x_ref[...] + jnp.dot(mix.astype(BF16), wo_ref[...], preferred_element_type=F32)
    h_o[...] = h
    r = lax.rsqrt(jnp.mean(h * h, axis=-1, keepdims=True) + RMS_EPS)
    hn = (h * r) * g_ref[...]
    hn_hi = hn.astype(BF16)
    hn_o[...] = hn_hi
    hn_lo = (hn - hn_hi.astype(F32)).astype(BF16)
    logits = (jnp.dot(hn_hi, wr_hi_ref[...], preferred_element_type=F32)
              + jnp.dot(hn_lo, wr_hi_ref[...], preferred_element_type=F32)
              + jnp.dot(hn_hi, wr_lo_ref[...], preferred_element_type=F32)) + br_ref[...]

    lane = lax.broadcasted_iota(jnp.int32, (tm, LANES), 1)
    is_g = (lane >= N_EXPERTS) & (lane < N_EXPERTS + N_GROUPS)
    gl = jnp.where(is_g, logits, -jnp.inf)
    gmax = jnp.max(gl, axis=1, keepdims=True)
    g_lane = jnp.min(jnp.where(gl == gmax, lane, 2 ** 30), axis=1, keepdims=True)
    g_w = 1.0 / jnp.sum(jnp.exp(gl - gmax), axis=1, keepdims=True)
    in_grp = (lane < N_EXPERTS) & ((lane // EXPERTS_PER_GROUP) == (g_lane - N_EXPERTS))
    e1 = jnp.where(in_grp, logits, -jnp.inf)
    v1 = jnp.max(e1, axis=1, keepdims=True)
    i1 = jnp.min(jnp.where(e1 == v1, lane, 2 ** 30), axis=1, keepdims=True)
    e2 = jnp.where(lane == i1, -jnp.inf, e1)
    v2 = jnp.max(e2, axis=1, keepdims=True)
    i2 = jnp.min(jnp.where(e2 == v2, lane, 2 ** 30), axis=1, keepdims=True)
    t2 = jnp.exp(v2 - v1)
    den = 1.0 + t2
    comb_o[...] = jnp.where(lane == i1, (1.0 / den) * g_w,
                            jnp.where(lane == i2, (t2 / den) * g_w, 0.0))


def _post(oa, ob, sga, sgb, x2d, wa16, wb16, wo16, g_ffn, wr_hi, wr_lo, br):
    n = x2d.shape[0]
    tm = min(256, n)
    assert n % tm == 0
    row = lambda i: (i, 0)
    fix = lambda i: (0, 0)
    return pl.pallas_call(
        functools.partial(_post_kernel, tm=tm),
        grid=(n // tm,),
        in_specs=[pl.BlockSpec((tm, D_HEADS), row), pl.BlockSpec((tm, D_HEADS), row),
                  pl.BlockSpec((tm, D_MODEL), row), pl.BlockSpec((tm, D_MODEL), row),
                  pl.BlockSpec((tm, D_MODEL), row),
                  pl.BlockSpec((D_HEADS, D_MODEL), fix), pl.BlockSpec((D_HEADS, D_MODEL), fix),
                  pl.BlockSpec((D_MODEL, D_MODEL), fix), pl.BlockSpec((1, D_MODEL), fix),
                  pl.BlockSpec((D_MODEL, LANES), fix), pl.BlockSpec((D_MODEL, LANES), fix),
                  pl.BlockSpec((1, LANES), fix)],
        out_specs=[pl.BlockSpec((tm, D_MODEL), row), pl.BlockSpec((tm, D_MODEL), row),
                   pl.BlockSpec((tm, LANES), row)],
        out_shape=[jax.ShapeDtypeStruct((n, D_MODEL), F32),
                   jax.ShapeDtypeStruct((n, D_MODEL), BF16),
                   jax.ShapeDtypeStruct((n, LANES), F32)],
        compiler_params=pltpu.CompilerParams(dimension_semantics=("arbitrary",),
                                             vmem_limit_bytes=VMEM_LIMIT),
        name="post",
    )(oa, ob, sga, sgb, x2d, wa16, wb16, wo16, g_ffn, wr_hi, wr_lo, br)


def _moe_kernel(hn_ref, comb_ref, h_ref, wg_ref, wu_ref, wd_ref, gf_ref, y_o, acc_ref, *, tm):
    e = pl.program_id(1)

    @pl.when(e == 0)
    def _init():
        acc_ref[...] = jnp.zeros(acc_ref.shape, F32)

    hn = hn_ref[...]
    gate = jnp.dot(hn, wg_ref[0].astype(BF16), preferred_element_type=F32)
    up = jnp.dot(hn, wu_ref[0].astype(BF16), preferred_element_type=F32)
    lane = lax.broadcasted_iota(jnp.int32, (tm, LANES), 1)
    c_e = jnp.sum(jnp.where(lane == e, comb_ref[...], 0.0), axis=1, keepdims=True)
    hid = (gate * jax.nn.sigmoid(gate)) * up * c_e
    acc_ref[...] += jnp.dot(hid.astype(BF16), wd_ref[0].astype(BF16), preferred_element_type=F32)

    @pl.when(e == N_EXPERTS - 1)
    def _finish():
        out = h_ref[...] + acc_ref[...]
        r = lax.rsqrt(jnp.mean(out * out, axis=-1, keepdims=True) + RMS_EPS)
        y_o[...] = (out * r) * gf_ref[...]


def _moe(hn16, comb, h, wg16, wu16, wd16, g_final):
    n = h.shape[0]
    tm = min(1024, n)
    assert n % tm == 0
    row = lambda i, e: (i, 0)
    exp = lambda i, e: (e, 0, 0)
    fix = lambda i, e: (0, 0)
    return pl.pallas_call(
        functools.partial(_moe_kernel, tm=tm),
        grid=(n // tm, N_EXPERTS),
        in_specs=[pl.BlockSpec((tm, D_MODEL), row), pl.BlockSpec((tm, LANES), row),
                  pl.BlockSpec((tm, D_MODEL), row),
                  pl.BlockSpec((1, D_MODEL, D_FF_EXPERT), exp),
                  pl.BlockSpec((1, D_MODEL, D_FF_EXPERT), exp),
                  pl.BlockSpec((1, D_FF_EXPERT, D_MODEL), exp),
                  pl.BlockSpec((1, D_MODEL), fix)],
        out_specs=pl.BlockSpec((tm, D_MODEL), row),
        out_shape=jax.ShapeDtypeStruct((n, D_MODEL), F32),
        scratch_shapes=[pltpu.VMEM((tm, D_MODEL), F32)],
        compiler_params=pltpu.CompilerParams(dimension_semantics=("arbitrary", "arbitrary"),
                                             vmem_limit_bytes=VMEM_LIMIT),
        name="moe",
    )(hn16, comb, h, wg16, wu16, wd16, g_final)


def _prep_weights(norm_mix_g, w_in, w_br_a, w_br_b, w_out, norm_ffn_g, w_rg, b_rg, w_re, b_re,
                  w_eg, w_eu, w_ed, norm_final_g):
    offs = np.cumsum(IN_SPLIT)[:-1].tolist()
    q_a, k_a, v_a, q_i, k_i, w_i, q_b, k_b, v_b, g_a, g_b = jnp.split(w_in, offs, axis=-1)
    w_i_pad = jnp.pad(w_i, ((0, 0), (0, LANES - N_IDX_HEADS)))
    w16 = jnp.concatenate([q_a, k_a, v_a, q_i, k_i, k_i, w_i_pad, q_b, k_b, v_b, g_a, g_b],
                          axis=1).astype(BF16)
    half = ROT_HALF
    inv_freq = ROPE_THETA ** (-jnp.arange(half, dtype=F32) / half)
    d = np.arange(LANES) % HEAD_DIM
    invf = jnp.where(jnp.asarray(d < ROT_DIM), inv_freq[jnp.asarray(d % half)], 0.0)[None, :]
    pad = LANES - N_EXPERTS - N_GROUPS
    w_r = jnp.pad(jnp.concatenate([w_re, w_rg], axis=1), ((0, 0), (0, pad)))
    wr_hi = w_r.astype(BF16)
    wr_lo = (w_r - wr_hi.astype(F32)).astype(BF16)
    b_r = jnp.pad(jnp.concatenate([b_re, b_rg]), (0, pad))[None, :]
    return dict(g_mix=norm_mix_g[None, :], w16=w16, invf=invf,
                wa16=w_br_a.astype(BF16), wb16=w_br_b.astype(BF16), wo16=w_out.astype(BF16),
                g_ffn=norm_ffn_g[None, :], wr_hi=wr_hi, wr_lo=wr_lo, b_r=b_r,
                wg=w_eg, wu=w_eu, wd=w_ed,
                g_final=norm_final_g[None, :])


def _layer(x, past, wts, *, dsa_qb, dsa_kt, sb_qb):
    bsz, t, _ = x.shape
    n = bsz * t
    x2d = x.reshape(n, D_MODEL)
    pos0 = 0 if past is None else past[0].shape[1]
    qa_scale = ATT_SCALE * LOG2_E if past is None else ATT_SCALE
    pr = _proj(x2d, wts["g_mix"], wts["w16"], wts["invf"], tq=t, pos0=pos0, qa_scale=qa_scale)
    shp = lambda a: a.reshape(bsz, t, a.shape[-1])

    if past is None:
        vt = pr["va16"].reshape(bsz, t, N_HEADS, HEAD_DIM).transpose(0, 2, 3, 1)
        vt = jnp.concatenate([vt, jnp.ones((bsz, N_HEADS, 1, t), BF16),
                              jnp.zeros((bsz, N_HEADS, LANES - HEAD_DIM - 1, t), BF16)], axis=2)
        oa = _dsat(shp(pr["qa"]), shp(pr["qi"]), shp(pr["wi"]), shp(pr["ki16"]), shp(pr["ka16"]),
                   vt.reshape(bsz, N_HEADS * LANES, t), pos0=0, l_valid=t,
                   qb_rows=dsa_qb, kt_rows=dsa_kt)
        ob = _sb(shp(pr["qb"]), shp(pr["kb16"]), shp(pr["vb16"]), pos0=0, qb_rows=sb_qb)
    else:
        keys_last = lambda c: jnp.transpose(c, (0, 2, 3, 1))
        oa = _dsac(shp(pr["qa"]), shp(pr["qi"]), shp(pr["wi"]), jnp.swapaxes(past[2], 1, 2),
                   shp(pr["ki16"]), keys_last(past[0]), keys_last(past[1]),
                   shp(pr["ka16"]), shp(pr["va16"]), kt_rows=dsa_kt)
        ob = _sbc(shp(pr["qb"]), shp(pr["kb16"]), shp(pr["vb16"]),
                  keys_last(past[3]), keys_last(past[4]))
    h, hn16, comb = _post(oa.reshape(n, D_HEADS), ob.reshape(n, D_HEADS), pr["sga"], pr["sgb"],
                          x2d, wts["wa16"], wts["wb16"], wts["wo16"], wts["g_ffn"],
                          wts["wr_hi"], wts["wr_lo"], wts["b_r"])
    y = _moe(hn16, comb, h, wts["wg"], wts["wu"], wts["wd"], wts["g_final"])
    heads = lambda a: a.reshape(1, bsz, t, N_HEADS, HEAD_DIM)
    rows = (heads(pr["ka"]), heads(pr["va"]),
            pr["ki"][:, :IDX_DIM].reshape(1, bsz, t, IDX_DIM),
            heads(pr["kb"]), heads(pr["vb"]))
    return y.reshape(bsz, t, D_MODEL), rows


def kernel(x_prompt, x_sample, cache_a_k, cache_a_v, cache_idx_k, cache_b_k, cache_b_v,
           norm_mix_g, w_in, w_br_a, w_br_b, w_out, norm_ffn_g,
           w_router_group, b_router_group, w_router_expert, b_router_expert,
           w_exp_gate, w_exp_up, w_exp_down, norm_final_g):
    assert w_in.shape[0] == 1, "single-layer model"
    wts = _prep_weights(norm_mix_g[0], w_in[0], w_br_a[0], w_br_b[0], w_out[0], norm_ffn_g[0],
                        w_router_group[0], b_router_group[0], w_router_expert[0],
                        b_router_expert[0], w_exp_gate[0], w_exp_up[0], w_exp_down[0],
                        norm_final_g)
    y_p, rows_p = _layer(x_prompt, None, wts, dsa_qb=256, dsa_kt=1024, sb_qb=128)
    past = (cache_a_k[0], cache_a_v[0], cache_idx_k[0], cache_b_k[0], cache_b_v[0])
    y_s, rows_s = _layer(x_sample, past, wts, dsa_qb=None, dsa_kt=1024, sb_qb=None)
    return (y_p, y_s) + rows_p + rows_s
```

```python
import functools

import numpy as np
import jax
import jax.numpy as jnp
from jax import lax
from jax.experimental import pallas as pl
from jax.experimental.pallas import tpu as pltpu

D_MODEL = 1024
HEAD_DIM = 64
N_HEADS = 8
D_HEADS = N_HEADS * HEAD_DIM
N_IDX_HEADS = 8
IDX_DIM = 64
IDX_SCALE = (N_IDX_HEADS * IDX_DIM) ** -0.5
ATT_SCALE = HEAD_DIM ** -0.5
LOG2_E = 1.4426950408889634
CHUNK = 64
TOPK_MAX = 256
ROPE_THETA = 500000.0
ROT_DIM = HEAD_DIM // 4
ROT_HALF = ROT_DIM // 2
N_GROUPS = 4
EXPERTS_PER_GROUP = 8
N_EXPERTS = N_GROUPS * EXPERTS_PER_GROUP
D_FF_EXPERT = D_MODEL // 4
RMS_EPS = 1e-6
IN_SPLIT = (D_HEADS, D_HEADS, D_HEADS, N_IDX_HEADS * IDX_DIM, IDX_DIM, N_IDX_HEADS,
            D_HEADS, D_HEADS, D_HEADS, D_MODEL, D_MODEL)

LANES = 128
SUBLANES = 8
N_PAIRS = N_HEADS // 2
HEADS_PER_STAGE = 8
NEG_BIG = -1e30
SB_DEAD = -110.0
VMEM_LIMIT = 56 * 1024 * 1024

C_QA, C_KA, C_VA, C_QI, C_KI, C_WI, C_QB, C_KB, C_VB, C_GA, C_GB, C_END = (
    0, 512, 1024, 1536, 2048, 2176, 2304, 2816, 3328, 3840, 4864, 5888)

F32 = jnp.float32
BF16 = jnp.bfloat16


def _nt_dot(a, b):
    return lax.dot_general(a, b, (((1,), (1,)), ((), ())), preferred_element_type=F32)


def _reduce0(x, op, ways=8):
    n = x.shape[0]
    if n % ways != 0 or n <= ways:
        ways = 1
    part = x.reshape(n // ways, ways, *x.shape[1:])
    acc = part[0]
    for j in range(1, n // ways):
        acc = op(acc, part[j])
    out = acc[0]
    for j in range(1, ways):
        out = op(out, acc[j])
    return out


def _head_halves(x):
    lane = lax.broadcasted_iota(jnp.int32, x.shape, 1)
    zero = jnp.zeros_like(x)
    return jnp.where(lane < HEAD_DIM, x, zero), jnp.where(lane >= HEAD_DIM, x, zero)


def _proj_kernel(x_ref, g_ref, w_ref, invf_ref,
                 qa_o, qi_o, qb_o, ka_o, va_o, kb_o, vb_o, ki_o, wi_o,
                 ka16_o, va16_o, kb16_o, vb16_o, ki16_o, sga_o, sgb_o,
                 *, tm, tq, pos0, qa_scale):
    x = x_ref[...]
    r = lax.rsqrt(jnp.mean(x * x, axis=-1, keepdims=True) + RMS_EPS)
    xn = (x * r) * g_ref[...]
    p = jnp.dot(xn.astype(BF16), w_ref[...], preferred_element_type=F32)

    row = pl.program_id(0) * tm + lax.broadcasted_iota(jnp.int32, (tm, 1), 0)
    pos = (row % tq + pos0).astype(F32)
    ang = pos * invf_ref[...]
    c = jnp.cos(ang)
    s = jnp.sin(ang)
    d = lax.broadcasted_iota(jnp.int32, (1, LANES), 1) % HEAD_DIM
    s_lo = jnp.where(d < ROT_HALF, -s, 0.0)
    s_hi = jnp.where((d >= ROT_HALF) & (d < ROT_DIM), s, 0.0)

    def rope(v):
        return (v * c + pltpu.roll(v, LANES - ROT_HALF, 1) * s_lo
                + pltpu.roll(v, ROT_HALF, 1) * s_hi)

    for j in range(D_HEADS // LANES):
        sl = slice(j * LANES, (j + 1) * LANES)
        qa = rope(p[:, C_QA + j * LANES:C_QA + (j + 1) * LANES])
        qa_o[:, sl] = (qa * qa_scale).astype(BF16)
        ka = rope(p[:, C_KA + j * LANES:C_KA + (j + 1) * LANES])
        ka_o[:, sl] = ka
        ka16_o[:, sl] = ka.astype(BF16)
        qi = rope(p[:, C_QI + j * LANES:C_QI + (j + 1) * LANES])
        qi_o[:, sl] = qi.astype(BF16)
    ki = rope(p[:, C_KI:C_KI + LANES])
    ki_o[...] = ki
    ki16_o[...] = ki.astype(BF16)
    wi_o[...] = p[:, C_WI:C_WI + LANES]
    va = p[:, C_VA:C_VA + D_HEADS]
    va_o[...] = va
    va16_o[...] = va.astype(BF16)
    qb_o[...] = (p[:, C_QB:C_QB + D_HEADS] * ATT_SCALE).astype(BF16)
    kb = p[:, C_KB:C_KB + D_HEADS]
    kb_o[...] = kb
    kb16_o[...] = kb.astype(BF16)
    vb = p[:, C_VB:C_VB + D_HEADS]
    vb_o[...] = vb
    vb16_o[...] = vb.astype(BF16)
    sga_o[...] = jax.nn.sigmoid(p[:, C_GA:C_GA + D_MODEL])
    sgb_o[...] = jax.nn.sigmoid(p[:, C_GB:C_GB + D_MODEL])


def _proj(x2d, g, w16, invf, *, tq, pos0, qa_scale):
    n = x2d.shape[0]
    tm = min(256, n)
    assert n % tm == 0
    row = lambda i: (i, 0)
    fix = lambda i: (0, 0)
    widths = dict(qa=D_HEADS, qi=D_HEADS, qb=D_HEADS, ka=D_HEADS, va=D_HEADS, kb=D_HEADS,
                  vb=D_HEADS, ki=LANES, wi=LANES, ka16=D_HEADS, va16=D_HEADS, kb16=D_HEADS,
                  vb16=D_HEADS, ki16=LANES, sga=D_MODEL, sgb=D_MODEL)
    dtypes = dict(qa=BF16, qi=BF16, qb=BF16, ka16=BF16, va16=BF16, kb16=BF16, vb16=BF16, ki16=BF16)
    names = list(widths)
    out_shape = [jax.ShapeDtypeStruct((n, widths[k]), dtypes.get(k, F32)) for k in names]
    out_specs = [pl.BlockSpec((tm, widths[k]), row) for k in names]
    outs = pl.pallas_call(
        functools.partial(_proj_kernel, tm=tm, tq=tq, pos0=pos0, qa_scale=qa_scale),
        grid=(n // tm,),
        in_specs=[pl.BlockSpec((tm, D_MODEL), row),
                  pl.BlockSpec((1, D_MODEL), fix),
                  pl.BlockSpec((D_MODEL, C_END), fix),
                  pl.BlockSpec((1, LANES), fix)],
        out_specs=out_specs,
        out_shape=out_shape,
        compiler_params=pltpu.CompilerParams(dimension_semantics=("arbitrary",),
                                             vmem_limit_bytes=VMEM_LIMIT),
        name="proj",
    )(x2d, g, w16, invf)
    return dict(zip(names, outs))


def _dsac_kernel(qa_ref, qi_ref, wi_ref, kic_ref, kin_ref, kc_ref, vc_ref, kn_ref, vn_ref, o_ref,
                 score_ref, thr_ref, jcut_ref, m_ref, l_ref, acc_ref,
                 *, n_ct, kt_rows, t_new, p_len, topk):
    kt = pl.program_id(1)
    qb_rows = t_new
    ch = kt_rows
    n_ch = n_ct + 1
    l_valid = p_len + t_new

    row = lax.broadcasted_iota(jnp.int32, (qb_rows, 1), 0)
    limit = jnp.minimum(((p_len + row) // CHUNK + 1) * CHUNK, l_valid)

    def pad_keys(x):
        return jnp.concatenate([x, jnp.zeros((LANES - t_new, HEAD_DIM), x.dtype)], axis=0)

    def count_where(pred_fn):
        def body(t, acc):
            blk = score_ref[t]
            for c in range(ch // LANES):
                acc = acc + jnp.where(pred_fn(t, blk[:, c * LANES:(c + 1) * LANES], c), 1.0, 0.0)
            return acc
        acc = lax.fori_loop(0, n_ch, body, jnp.zeros((qb_rows, LANES), F32))
        return jnp.sum(acc, axis=1, keepdims=True)

    @pl.when(kt == 0)
    def _select():
        qi = qi_ref[0]
        q_idx = [qi[:, h * IDX_DIM:(h + 1) * IDX_DIM] for h in range(N_IDX_HEADS)]
        w = wi_ref[0] * IDX_SCALE
        w_cols = [w[:, h:h + 1] for h in range(N_IDX_HEADS)]

        def scores(rel_of_head, n_keys):
            s = jnp.zeros((qb_rows, n_keys), F32)
            for h in range(N_IDX_HEADS):
                s = s + w_cols[h] * jnp.maximum(rel_of_head(h), 0.0)
            return s

        for t in range(n_ct):
            kk_t = kic_ref[0, :, t * ch:(t + 1) * ch].astype(BF16)
            kpos = t * ch + lax.broadcasted_iota(jnp.int32, (1, ch), 1)
            s = scores(lambda h: jnp.dot(q_idx[h], kk_t, preferred_element_type=F32), ch)
            score_ref[t] = jnp.where(kpos < limit, s, -jnp.inf)
        kk_new = pad_keys(kin_ref[0][:, :IDX_DIM])
        s_new = scores(lambda h: _nt_dot(q_idx[h], kk_new), LANES)
        kpos = p_len + lax.broadcasted_iota(jnp.int32, (1, LANES), 1)
        score_ref[n_ct] = jnp.concatenate(
            [jnp.where(kpos < limit, s_new, -jnp.inf),
             jnp.full((qb_rows, ch - LANES), -jnp.inf, F32)], axis=1)

        def minmax_body(t, carry):
            mn, mx = carry
            blk = score_ref[t]
            for c in range(ch // LANES):
                piece = blk[:, c * LANES:(c + 1) * LANES]
                mx = jnp.maximum(mx, piece)
                mn = jnp.minimum(mn, jnp.where(piece == -jnp.inf, jnp.inf, piece))
            return mn, mx
        mn, mx = lax.fori_loop(0, n_ch, minmax_body,
                               (jnp.full((qb_rows, LANES), jnp.inf, F32),
                                jnp.full((qb_rows, LANES), -jnp.inf, F32)))
        lo0 = jnp.min(mn, axis=1, keepdims=True)
        smax = jnp.max(mx, axis=1, keepdims=True)
        hi0 = smax + (jnp.abs(smax) * 1e-6 + 1e-30)
        all_in = limit <= topk
        done0 = jnp.where(all_in, 1.0, 0.0)
        thr0 = jnp.where(all_in, jnp.float32(-3e38), lo0)
        zeros = jnp.zeros((qb_rows, 1), F32)

        def cond(c):
            return c[0] > 0

        def bisect_pass(lo, hi, chi, thr, done, tie, cand, pend):
            half = lo + 0.5 * (hi - lo)
            has_cand = pend > 0.5
            mid = jnp.where(has_cand, cand, half)
            stuck = jnp.logical_not(has_cand) & ((half <= lo) | (half >= hi))
            mid_b = jnp.broadcast_to(mid, (qb_rows, LANES))
            cnt = count_where(lambda t, piece, cidx: piece >= mid_b)
            active = done < 0.5
            moving = active & jnp.logical_not(stuck)
            found = moving & ((cnt == topk) | (has_cand & (cnt > topk)))
            now_tie = (active & stuck) | (moving & has_cand & (cnt > topk))
            up = moving & jnp.logical_not(has_cand) & (cnt > topk)
            down = moving & (cnt < topk)
            thr = jnp.where(found, mid, jnp.where(active & stuck, lo, thr))
            tie = jnp.where(now_tie, 1.0, tie)
            done = jnp.where(found | now_tie, 1.0, done)
            lo = jnp.where(up, mid, lo)
            chi = jnp.where(down, cnt, chi)
            hi = jnp.where(down, mid, hi)
            return lo, hi, chi, thr, done, tie, cand, jnp.zeros_like(pend)

        def snap_pass(lo, hi, chi, thr, done, tie, cand, pend):
            hi_b = jnp.broadcast_to(hi, (qb_rows, LANES))

            def body(t, acc):
                blk = score_ref[t]
                for c in range(ch // LANES):
                    piece = blk[:, c * LANES:(c + 1) * LANES]
                    acc = jnp.maximum(acc, jnp.where(piece < hi_b, piece, -jnp.inf))
                return acc
            acc = lax.fori_loop(0, n_ch, body, jnp.full((qb_rows, LANES), -jnp.inf, F32))
            cand = jnp.max(acc, axis=1, keepdims=True)
            return lo, hi, chi, thr, done, tie, cand, 1.0 - done

        def body(c):
            it = c[1]
            snap = (it >= 11) & (it % 4 == 3)
            new = lax.cond(snap, snap_pass, bisect_pass, *c[2:])
            left = jnp.sum(1.0 - new[4])
            go = jnp.where((left > 0) & (it < 400), 1, 0)
            return (go, it + 1) + tuple(new)

        left0 = jnp.sum(1.0 - done0)
        state = (jnp.where(left0 > 0, 1, 0), jnp.int32(0), lo0, hi0, zeros, thr0, done0, zeros,
                 zeros, zeros)
        _, _, _, _, chi, thr, _, tie, _, _ = lax.while_loop(cond, body, state)

        big = jnp.full((qb_rows, 1), 2 ** 30, jnp.int32)

        def tie_break():
            need = topk - chi
            thr_b = jnp.broadcast_to(thr, (qb_rows, LANES))
            lane1 = lax.broadcasted_iota(jnp.int32, (1, LANES), 1)

            def tb_body(_, c):
                lo_j, hi_j = c
                mid_j = lo_j + (hi_j - lo_j) // 2
                mid_jb = jnp.broadcast_to(mid_j, (qb_rows, LANES))
                cnt = count_where(lambda t, piece, cidx: (piece == thr_b)
                                  & (t * ch + cidx * LANES + lane1 <= mid_jb))
                ok = cnt >= need
                return jnp.where(ok, lo_j, mid_j), jnp.where(ok, mid_j, hi_j)

            lo_j = jnp.full((qb_rows, 1), -1, jnp.int32)
            hi_j = jnp.zeros((qb_rows, 1), jnp.int32) + (n_ch * ch - 1)
            n_steps = int(np.ceil(np.log2(score_ref.shape[0] * ch))) + 1
            _, hi_j = lax.fori_loop(0, n_steps, tb_body, (lo_j, hi_j))
            return jnp.where(tie > 0, hi_j, big)

        jcut = lax.cond(jnp.sum(tie) > 0, tie_break, lambda: big)
        thr_ref[...] = jnp.broadcast_to(thr, (qb_rows, LANES))
        jcut_ref[...] = jnp.broadcast_to(jcut, (qb_rows, LANES))
        m_ref[...] = jnp.full(m_ref.shape, NEG_BIG, F32)
        l_ref[...] = jnp.zeros(l_ref.shape, F32)
        acc_ref[...] = jnp.zeros(acc_ref.shape, F32)

    def update(n_keys, logits_of_head, pv_of_head):
        thr_b = thr_ref[...]
        jcut_b = jcut_ref[...]
        lane1 = lax.broadcasted_iota(jnp.int32, (1, LANES), 1)
        blk = score_ref[kt]
        bias = []
        for c in range(n_keys // LANES):
            piece = blk[:, c * LANES:(c + 1) * LANES]
            kpos = kt * kt_rows + c * LANES + lane1
            sel = (piece >= thr_b) & ((piece > thr_b) | (kpos <= jcut_b))
            bias.append(jnp.where(sel, 0.0, NEG_BIG))
        n_piece = len(bias)
        qa = qa_ref[0]
        heads = range(N_HEADS)

        def fold(xs, op):
            out = xs[0]
            for x in xs[1:]:
                out = op(out, x)
            return out

        logits = [logits_of_head(h, qa[:, h * HEAD_DIM:(h + 1) * HEAD_DIM]) for h in heads]
        pieces = [[logits[h][:, c * LANES:(c + 1) * LANES] + bias[c] for c in range(n_piece)]
                  for h in heads]
        m_old = [m_ref[h] for h in heads]
        m_new = [jnp.maximum(m_old[h], jnp.max(fold(pieces[h], jnp.maximum), axis=1, keepdims=True))
                 for h in heads]
        alpha = [jnp.exp(m_old[h] - m_new[h]) for h in heads]
        ps = [[jnp.exp(pc - m_new[h]) for pc in pieces[h]] for h in heads]
        pv = [pv_of_head(h, jnp.concatenate([pc.astype(BF16) for pc in ps[h]], axis=1))
              for h in heads]
        for h in heads:
            m_ref[h] = m_new[h]
            l_ref[h] = alpha[h] * l_ref[h] + fold(ps[h], jnp.add)
            acc_ref[h] = alpha[h][:, :HEAD_DIM] * acc_ref[h] + pv[h]

    @pl.when(kt < n_ct)
    def _cache_tile():
        update(kt_rows,
               lambda h, q_h: jnp.dot(q_h, kc_ref[0, h].astype(BF16), preferred_element_type=F32),
               lambda h, p16: _nt_dot(p16, vc_ref[0, h].astype(BF16)))

    @pl.when(kt == n_ct)
    def _new_tile():
        kn = kn_ref[0]
        vn = vn_ref[0]
        head = lambda x, h: pad_keys(x[:, h * HEAD_DIM:(h + 1) * HEAD_DIM])
        update(LANES,
               lambda h, q_h: _nt_dot(q_h, head(kn, h)),
               lambda h, p16: jnp.dot(p16, head(vn, h), preferred_element_type=F32))
        for h in range(N_HEADS):
            denom = jnp.sum(l_ref[h], axis=1, keepdims=True)
            o_ref[0, :, h * HEAD_DIM:(h + 1) * HEAD_DIM] = (acc_ref[h] / denom).astype(o_ref.dtype)


def _dsat_kernel(qb_of, kt_of, first_of, last_of,
                 qa_ref, qi_ref, wi_ref, ki_ref, k_ref, vt_ref, o_ref,
                 score_ref, m_ref, acc_ref,
                 *, qb_rows, kt_rows, pos0, l_valid, topk):
    i = pl.program_id(1)
    qb = qb_of[i]
    kt = kt_of[i]
    nq = qb_rows
    ns = kt_rows // SUBLANES

    qcol = qb * nq + lax.broadcasted_iota(jnp.int32, (SUBLANES, nq), 1)
    limit = jnp.minimum(((pos0 + qcol) // CHUNK + 1) * CHUNK, l_valid)
    max_limit = jnp.minimum(((pos0 + qb * nq + nq - 1) // CHUNK + 1) * CHUNK, l_valid)
    n_kt = (max_limit + kt_rows - 1) // kt_rows
    kidx = (lax.broadcasted_iota(jnp.int32, (ns, SUBLANES, nq), 0) * SUBLANES
            + lax.broadcasted_iota(jnp.int32, (ns, SUBLANES, nq), 1))

    def reduce_tiles(fn, op, init):
        def body(t, acc):
            return op(acc, fn(t, score_ref[t]))
        return lax.fori_loop(0, n_kt, body, init)

    def count_where(pred):
        acc = reduce_tiles(lambda t, blk: _reduce0(jnp.where(pred(t, blk), 1.0, 0.0), jnp.add),
                           jnp.add, jnp.zeros((SUBLANES, nq), F32))
        return jnp.broadcast_to(jnp.sum(acc, axis=0, keepdims=True), (SUBLANES, nq))

    @pl.when(first_of[i] == 1)
    def _select():
        qi = qi_ref[0]
        q_halves = []
        for g in range(N_PAIRS):
            q_halves.extend(_head_halves(qi[:, g * LANES:(g + 1) * LANES]))
        w_t = wi_ref[0].T * IDX_SCALE
        w_rows = [jnp.broadcast_to(w_t[h:h + 1, :], (SUBLANES, nq)) for h in range(N_IDX_HEADS)]

        def score_body(t, carry):
            mn, mx, n_pos, n_nonneg = carry
            kk = ki_ref[0, pl.ds(pl.multiple_of(t * kt_rows, kt_rows), kt_rows), :]
            s = jnp.zeros((ns, SUBLANES, nq), F32)
            for h in range(N_IDX_HEADS):
                rel = jnp.maximum(_nt_dot(kk, q_halves[h]), 0.0)
                s = s + w_rows[h] * rel.reshape(ns, SUBLANES, nq)
            seen = kidx + t * kt_rows < limit
            s_lo = jnp.where(seen, s, -jnp.inf)
            score_ref[t] = s_lo
            mn = jnp.minimum(mn, _reduce0(jnp.where(seen, s, jnp.inf), jnp.minimum))
            mx = jnp.maximum(mx, _reduce0(s_lo, jnp.maximum))
            n_pos = n_pos + _reduce0(jnp.where(s_lo > 0.0, 1.0, 0.0), jnp.add)
            n_nonneg = n_nonneg + _reduce0(jnp.where(s_lo >= 0.0, 1.0, 0.0), jnp.add)
            return mn, mx, n_pos, n_nonneg
        zeros = jnp.zeros((SUBLANES, nq), F32)
        mn8, mx8, n_pos8, n_nonneg8 = lax.fori_loop(
            0, n_kt, score_body,
            (jnp.full((SUBLANES, nq), jnp.inf, F32), jnp.full((SUBLANES, nq), -jnp.inf, F32),
             zeros, zeros))

        rep = lambda v: jnp.broadcast_to(v, (SUBLANES, nq))
        lo0 = rep(jnp.min(mn8, axis=0, keepdims=True))
        smax = rep(jnp.max(mx8, axis=0, keepdims=True))
        hi0 = smax + (jnp.abs(smax) * 1e-6 + 1e-30)
        n_pos = rep(jnp.sum(n_pos8, axis=0, keepdims=True))
        n_nonneg = rep(jnp.sum(n_nonneg8, axis=0, keepdims=True))
        all_in = limit <= topk
        below0 = jnp.logical_not(all_in) & (n_nonneg < topk)
        at0 = jnp.logical_not(all_in) & (n_pos < topk) & (n_nonneg >= topk)
        above0 = jnp.logical_not(all_in) & (n_pos >= topk)
        hi0 = jnp.where(below0, 0.0, hi0)
        chi0 = jnp.where(below0, n_nonneg, jnp.where(at0, n_pos, 0.0))
        lo0 = jnp.where(above0, 0.0, lo0)
        done0 = jnp.where(all_in | at0, 1.0, 0.0)
        tie0 = jnp.where(at0 & (n_nonneg > topk), 1.0, 0.0)
        thr0 = jnp.where(all_in, jnp.float32(-3e38), jnp.where(at0, 0.0, lo0))

        def bisect_pass(lo, hi, chi, thr, done, tie, cand, pend):
            half = lo + 0.5 * (hi - lo)
            has_cand = pend > 0.5
            mid = jnp.where(has_cand, cand, half)
            stuck = jnp.logical_not(has_cand) & ((half <= lo) | (half >= hi))
            cnt = count_where(lambda t, blk: blk >= mid)
            active = done < 0.5
            moving = active & jnp.logical_not(stuck)
            found = moving & ((cnt == topk) | (has_cand & (cnt > topk)))
            now_tie = (active & stuck) | (moving & has_cand & (cnt > topk))
            up = moving & jnp.logical_not(has_cand) & (cnt > topk)
            down = moving & (cnt < topk)
            thr = jnp.where(found, mid, jnp.where(active & stuck, lo, thr))
            tie = jnp.where(now_tie, 1.0, tie)
            done = jnp.where(found | now_tie, 1.0, done)
            lo = jnp.where(up, mid, lo)
            chi = jnp.where(down, cnt, chi)
            hi = jnp.where(down, mid, hi)
            return lo, hi, chi, thr, done, tie, cand, jnp.zeros_like(pend)

        def snap_pass(lo, hi, chi, thr, done, tie, cand, pend):
            below = reduce_tiles(
                lambda t, blk: _reduce0(jnp.where(blk < hi, blk, -jnp.inf), jnp.maximum),
                jnp.maximum, jnp.full((SUBLANES, nq), -jnp.inf, F32))
            cand = rep(jnp.max(below, axis=0, keepdims=True))
            return lo, hi, chi, thr, done, tie, cand, 1.0 - done

        def cond(c):
            return c[0] > 0

        def body(c):
            it = c[1]
            snap = (it >= 11) & (it % 4 == 3)
            new = lax.cond(snap, snap_pass, bisect_pass, *c[2:])
            left = jnp.sum(1.0 - new[4])
            go = jnp.where((left > 0) & (it < 400), 1, 0)
            return (go, it + 1) + tuple(new)

        left0 = jnp.sum(1.0 - done0)
        state = (jnp.where(left0 > 0, 1, 0), jnp.int32(0), lo0, hi0, chi0, thr0, done0, tie0,
                 zeros, zeros)
        _, _, _, _, chi, thr, _, tie, _, _ = lax.while_loop(cond, body, state)

        big = jnp.full((SUBLANES, nq), 2 ** 30, jnp.int32)

        def tie_break():
            need = topk - chi

            def tb_body(_, c):
                lo_j, hi_j = c
                mid_j = lo_j + (hi_j - lo_j) // 2
                cnt = count_where(lambda t, blk: (blk == thr) & (kidx + t * kt_rows <= mid_j))
                ok = cnt >= need
                return jnp.where(ok, lo_j, mid_j), jnp.where(ok, mid_j, hi_j)

            lo_j = jnp.full((SUBLANES, nq), -1, jnp.int32)
            hi_j = jnp.zeros((SUBLANES, nq), jnp.int32) + (n_kt * kt_rows - 1)
            n_steps = int(np.ceil(np.log2(score_ref.shape[0] * kt_rows))) + 1
            _, hi_j = lax.fori_loop(0, n_steps, tb_body, (lo_j, hi_j))
            return jnp.where(tie > 0, hi_j, big)

        jcut = lax.cond(jnp.sum(tie) > 0, tie_break, lambda: big)

        def mask_body(t, carry):
            blk = score_ref[t]
            sel = (blk >= thr) & ((blk > thr) | (kidx + t * kt_rows <= jcut))
            score_ref[t] = jnp.where(sel, 0.0, NEG_BIG)
            return carry
        lax.fori_loop(0, n_kt, mask_body, 0)

        m_ref[...] = jnp.full(m_ref.shape, NEG_BIG, F32)
        acc_ref[...] = jnp.zeros(acc_ref.shape, F32)

    bias = score_ref[kt]
    qa = qa_ref[0]
    q_heads = []
    for g in range(N_PAIRS):
        q_heads.extend(_head_halves(qa[:, g * LANES:(g + 1) * LANES]))
    for h0 in range(0, N_HEADS, HEADS_PER_STAGE):
        heads = range(h0, h0 + HEADS_PER_STAGE)
        logits = {h: _nt_dot(k_ref[0, :, (h // 2) * LANES:(h // 2 + 1) * LANES], q_heads[h])
                  .reshape(ns, SUBLANES, nq) + bias for h in heads}
        m_old = {h: m_ref[h] for h in heads}
        m_new = {h: jnp.maximum(m_old[h], jnp.max(_reduce0(logits[h], jnp.maximum),
                                                  axis=0, keepdims=True)) for h in heads}
        alpha = {h: jnp.exp2(m_old[h] - m_new[h]) for h in heads}
        p = {h: jnp.exp2(logits[h] - m_new[h]) for h in heads}
        pv = {h: jnp.dot(vt_ref[0, h * LANES:(h + 1) * LANES, :],
                         p[h].reshape(kt_rows, nq).astype(BF16), preferred_element_type=F32)
              for h in heads}
        for h in heads:
            acc_ref[h] = alpha[h][0:1] * acc_ref[h] + pv[h]
            m_ref[h] = m_new[h]

    @pl.when(last_of[i] == 1)
    def _finish():
        for g in range(N_PAIRS):
            outs = []
            for h in (2 * g, 2 * g + 1):
                acc = acc_ref[h]
                outs.append(acc[:HEAD_DIM] / acc[HEAD_DIM:HEAD_DIM + 1])
            o_t = jnp.concatenate(outs, axis=0)
            o_ref[0, :, g * LANES:(g + 1) * LANES] = o_t.T.astype(o_ref.dtype)


def _dsat(qa, qi, wi, ki16, k16, vt16, *, pos0, l_valid, qb_rows, kt_rows):
    bsz, tq, _ = qa.shape
    l_pad = k16.shape[1]
    assert tq % qb_rows == 0 and l_pad % kt_rows == 0 and qb_rows % LANES == 0
    topk = min(TOPK_MAX, l_valid // 4)
    tables = _dsa_tables(tq, qb_rows, kt_rows, pos0, l_valid)
    q_map = lambda b, i, qb_of, kt_of, f, l: (b, qb_of[i], 0)
    k_map = lambda b, i, qb_of, kt_of, f, l: (b, kt_of[i], 0)
    vt_map = lambda b, i, qb_of, kt_of, f, l: (b, 0, kt_of[i])
    all_map = lambda b, i, qb_of, kt_of, f, l: (b, 0, 0)
    grid_spec = pltpu.PrefetchScalarGridSpec(
        num_scalar_prefetch=4,
        grid=(bsz, tables[0].shape[0]),
        in_specs=[pl.BlockSpec((1, qb_rows, D_HEADS), q_map),
                  pl.BlockSpec((1, qb_rows, D_HEADS), q_map),
                  pl.BlockSpec((1, qb_rows, LANES), q_map),
                  pl.BlockSpec((1, l_pad, LANES), all_map),
                  pl.BlockSpec((1, kt_rows, D_HEADS), k_map),
                  pl.BlockSpec((1, N_HEADS * LANES, kt_rows), vt_map)],
        out_specs=pl.BlockSpec((1, qb_rows, D_HEADS), q_map),
        scratch_shapes=[pltpu.VMEM((l_pad // kt_rows, kt_rows // SUBLANES, SUBLANES, qb_rows), F32),
                        pltpu.VMEM((N_HEADS, SUBLANES, qb_rows), F32),
                        pltpu.VMEM((N_HEADS, LANES, qb_rows), F32)])
    return pl.pallas_call(
        functools.partial(_dsat_kernel, qb_rows=qb_rows, kt_rows=kt_rows, pos0=pos0,
                          l_valid=l_valid, topk=topk),
        grid_spec=grid_spec,
        out_shape=jax.ShapeDtypeStruct((bsz, tq, D_HEADS), BF16),
        compiler_params=pltpu.CompilerParams(dimension_semantics=("arbitrary", "arbitrary"),
                                             vmem_limit_bytes=VMEM_LIMIT),
        name="dsat",
    )(*tables, qa, qi, wi, ki16, k16, vt16)


def _dsa_tables(tq, qb_rows, kt_rows, pos0, l_valid):
    qb_l, kt_l, first_l, last_l = [], [], [], []
    for qb in range(tq // qb_rows):
        max_limit = min(((pos0 + qb * qb_rows + qb_rows - 1) // CHUNK + 1) * CHUNK, l_valid)
        n_kt = -(-max_limit // kt_rows)
        for kt in range(n_kt):
            qb_l.append(qb)
            kt_l.append(kt)
            first_l.append(int(kt == 0))
            last_l.append(int(kt == n_kt - 1))
    return [jnp.asarray(np.asarray(a, np.int32)) for a in (qb_l, kt_l, first_l, last_l)]


def _dsac(qa, qi, wi, ki_cache, ki_new, k_cache, v_cache, k_new, v_new, *, kt_rows):
    bsz, t_new, _ = qa.shape
    p_len = k_cache.shape[3]
    assert p_len % kt_rows == 0 and kt_rows % LANES == 0 and kt_rows > LANES
    assert t_new % 16 == 0 and t_new <= LANES
    n_ct = p_len // kt_rows
    topk = min(TOPK_MAX, (p_len + t_new) // 4)
    q_map = lambda b, j: (b, 0, 0)
    c_map = lambda b, j: (b, 0, 0, jnp.minimum(j, n_ct - 1))
    return pl.pallas_call(
        functools.partial(_dsac_kernel, n_ct=n_ct, kt_rows=kt_rows, t_new=t_new, p_len=p_len,
                          topk=topk),
        grid=(bsz, n_ct + 1),
        in_specs=[pl.BlockSpec((1, t_new, D_HEADS), q_map),
                  pl.BlockSpec((1, t_new, D_HEADS), q_map),
                  pl.BlockSpec((1, t_new, LANES), q_map),
                  pl.BlockSpec((1, IDX_DIM, p_len), q_map),
                  pl.BlockSpec((1, t_new, LANES), q_map),
                  pl.BlockSpec((1, N_HEADS, HEAD_DIM, kt_rows), c_map),
                  pl.BlockSpec((1, N_HEADS, HEAD_DIM, kt_rows), c_map),
                  pl.BlockSpec((1, t_new, D_HEADS), q_map),
                  pl.BlockSpec((1, t_new, D_HEADS), q_map)],
        out_specs=pl.BlockSpec((1, t_new, D_HEADS), q_map),
        out_shape=jax.ShapeDtypeStruct((bsz, t_new, D_HEADS), BF16),
        scratch_shapes=[pltpu.VMEM((n_ct + 1, t_new, kt_rows), F32),
                        pltpu.VMEM((t_new, LANES), F32),
                        pltpu.VMEM((t_new, LANES), jnp.int32),
                        pltpu.VMEM((N_HEADS, t_new, LANES), F32),
                        pltpu.VMEM((N_HEADS, t_new, LANES), F32),
                        pltpu.VMEM((N_HEADS, t_new, HEAD_DIM), F32)],
        compiler_params=pltpu.CompilerParams(dimension_semantics=("arbitrary", "arbitrary"),
                                             vmem_limit_bytes=VMEM_LIMIT),
        name="dsac",
    )(qa, qi, wi, ki_cache, ki_new, k_cache, v_cache, k_new, v_new)


def _sb_kernel(q_ref, k_ref, v_ref, o_ref, run_ref, acc_ref, *, qb_rows, pos0):
    qb = pl.program_id(1)
    qpos = pos0 + qb * qb_rows + lax.broadcasted_iota(jnp.int32, (qb_rows, 1), 0)
    kt_top = (pos0 + qb * qb_rows + qb_rows - 1) // LANES
    lane1 = lax.broadcasted_iota(jnp.int32, (1, LANES), 1)
    jj = lax.broadcasted_iota(jnp.int32, (LANES, 2 * LANES), 0)
    ss = lax.broadcasted_iota(jnp.int32, (LANES, 2 * LANES), 1)
    sum_rhs = jnp.where((jj > ss) | (ss >= LANES), 1.0, 0.0).astype(BF16)
    lane = lax.broadcasted_iota(jnp.int32, (qb_rows, LANES), 1)

    q = q_ref[0]
    q_heads = []
    for g in range(N_PAIRS):
        q_heads.extend(_head_halves(q[:, g * LANES:(g + 1) * LANES]))
    run_ref[...] = jnp.zeros(run_ref.shape, F32)
    acc_ref[...] = jnp.zeros(acc_ref.shape, F32)

    def cond(c):
        return c[0] > 0

    def body(c):
        _, kt = c
        rows = pl.ds(pl.multiple_of(kt * LANES, LANES), LANES)
        strict = (kt * LANES + lane1) < qpos
        heads = range(N_HEADS)
        k_t = [k_ref[0, rows, g * LANES:(g + 1) * LANES] for g in range(N_PAIRS)]
        v_t = [v_ref[0, rows, g * LANES:(g + 1) * LANES] for g in range(N_PAIRS)]
        z = [_nt_dot(q_heads[h], k_t[h // 2]) for h in heads]
        sp = [jnp.maximum(z[h], 0.0) + jnp.log1p(jnp.exp(-jnp.abs(z[h]))) for h in heads]
        ls = [jnp.where(strict, -sp[h], 0.0) for h in heads]
        ls_hi = [ls[h].astype(BF16) for h in heads]
        ls_lo = [(ls[h] - ls_hi[h].astype(F32)).astype(BF16) for h in heads]
        sums = [jnp.dot(ls_hi[h], sum_rhs, preferred_element_type=F32)
                + jnp.dot(ls_lo[h], sum_rhs, preferred_element_type=F32) for h in heads]
        run_old = [run_ref[h] for h in heads]
        wgt = [jnp.where(strict, jnp.exp(z[h] - sp[h] + run_old[h] + sums[h][:, :LANES]), 0.0)
               for h in heads]
        pv = [jnp.dot(wgt[h].astype(BF16), v_t[h // 2], preferred_element_type=F32) for h in heads]
        run_new = [run_old[h] + sums[h][:, LANES:] for h in heads]
        alive = run_new[0]
        for h in range(1, N_HEADS):
            alive = jnp.maximum(alive, run_new[h])
        for h in heads:
            acc_ref[h] += pv[h]
            run_ref[h] = run_new[h]
        go = jnp.where((kt > 0) & (jnp.max(alive) > SB_DEAD), 1, 0)
        return go, kt - 1

    lax.while_loop(cond, body, (jnp.int32(1), kt_top))
    for g in range(N_PAIRS):
        o_ref[0, :, g * LANES:(g + 1) * LANES] = jnp.where(
            lane < HEAD_DIM, acc_ref[2 * g], acc_ref[2 * g + 1]).astype(o_ref.dtype)


def _sb(qb16, k16, v16, *, pos0, qb_rows):
    bsz, tq, _ = qb16.shape
    l_pad = k16.shape[1]
    assert tq % qb_rows == 0 and l_pad % LANES == 0
    q_map = lambda b, i: (b, i, 0)
    all_map = lambda b, i: (b, 0, 0)
    resident = dict(pipeline_mode=pl.Buffered(1)) if bsz == 1 else {}
    return pl.pallas_call(
        functools.partial(_sb_kernel, qb_rows=qb_rows, pos0=pos0),
        grid=(bsz, tq // qb_rows),
        in_specs=[pl.BlockSpec((1, qb_rows, D_HEADS), q_map),
                  pl.BlockSpec((1, l_pad, D_HEADS), all_map, **resident),
                  pl.BlockSpec((1, l_pad, D_HEADS), all_map, **resident)],
        out_specs=pl.BlockSpec((1, qb_rows, D_HEADS), q_map),
        out_shape=jax.ShapeDtypeStruct((bsz, tq, D_HEADS), BF16),
        scratch_shapes=[pltpu.VMEM((N_HEADS, qb_rows, LANES), F32),
                        pltpu.VMEM((N_HEADS, qb_rows, LANES), F32)],
        compiler_params=pltpu.CompilerParams(dimension_semantics=("arbitrary", "arbitrary"),
                                             vmem_limit_bytes=VMEM_LIMIT),
        name="sb",
    )(qb16, k16, v16)


def _sbc_kernel(q_ref, kn_ref, vn_ref, kc_hbm, vc_hbm, o_ref, kbuf, vbuf, sem, run_ref, acc_ref,
                *, t_new, p_len):
    b = pl.program_id(0)
    n_ct = p_len // LANES
    qpos = p_len + lax.broadcasted_iota(jnp.int32, (t_new, 1), 0)
    lane1 = lax.broadcasted_iota(jnp.int32, (1, LANES), 1)
    jj = lax.broadcasted_iota(jnp.int32, (LANES, 2 * LANES), 0)
    ss = lax.broadcasted_iota(jnp.int32, (LANES, 2 * LANES), 1)
    sum_rhs = jnp.where((jj > ss) | (ss >= LANES), 1.0, 0.0).astype(BF16)

    def tile_copies(kt, slot):
        keys = pl.ds(pl.multiple_of(kt * LANES, LANES), LANES)
        return (pltpu.make_async_copy(kc_hbm.at[b, :, :, keys], kbuf.at[slot], sem.at[0, slot]),
                pltpu.make_async_copy(vc_hbm.at[b, :, :, keys], vbuf.at[slot], sem.at[1, slot]))

    for cp in tile_copies(n_ct - 1, 0):
        cp.start()

    q = q_ref[0]
    q_heads = [q[:, h * HEAD_DIM:(h + 1) * HEAD_DIM] for h in range(N_HEADS)]
    run_ref[...] = jnp.zeros(run_ref.shape, F32)
    acc_ref[...] = jnp.zeros(acc_ref.shape, F32)

    def tile_update(kt, z_of_head, pv_of_head):
        strict = (kt * LANES + lane1) < qpos
        heads = range(N_HEADS)
        z = [z_of_head(h) for h in heads]
        sp = [jnp.maximum(z[h], 0.0) + jnp.log1p(jnp.exp(-jnp.abs(z[h]))) for h in heads]
        ls = [jnp.where(strict, -sp[h], 0.0) for h in heads]
        ls_hi = [ls[h].astype(BF16) for h in heads]
        ls_lo = [(ls[h] - ls_hi[h].astype(F32)).astype(BF16) for h in heads]
        sums = [jnp.dot(ls_hi[h], sum_rhs, preferred_element_type=F32)
                + jnp.dot(ls_lo[h], sum_rhs, preferred_element_type=F32) for h in heads]
        run_old = [run_ref[h] for h in heads]
        wgt = [jnp.where(strict, jnp.exp(z[h] - sp[h] + run_old[h] + sums[h][:, :LANES]), 0.0)
               for h in heads]
        pv = [pv_of_head(h, wgt[h].astype(BF16)) for h in heads]
        run_new = [run_old[h] + sums[h][:, LANES:] for h in heads]
        alive = run_new[0]
        for h in range(1, N_HEADS):
            alive = jnp.maximum(alive, run_new[h])
        for h in heads:
            acc_ref[h] += pv[h]
            run_ref[h] = run_new[h]
        return jnp.max(alive)

    def new_head(x, h):
        return jnp.concatenate([x[:, h * HEAD_DIM:(h + 1) * HEAD_DIM],
                                jnp.zeros((LANES - t_new, HEAD_DIM), x.dtype)], axis=0)

    kn = kn_ref[0]
    vn = vn_ref[0]
    top = tile_update(n_ct,
                      lambda h: _nt_dot(q_heads[h], new_head(kn, h)),
                      lambda h, w16: jnp.dot(w16, new_head(vn, h), preferred_element_type=F32))

    def cond(c):
        return c[0] > 0

    def body(c):
        _, kt, slot, _ = c
        for cp in tile_copies(kt, slot):
            cp.wait()

        @pl.when(kt > 0)
        def _prefetch():
            for cp in tile_copies(kt - 1, 1 - slot):
                cp.start()

        top = tile_update(
            kt,
            lambda h: jnp.dot(q_heads[h], kbuf[slot, h].astype(BF16), preferred_element_type=F32),
            lambda h, w16: _nt_dot(w16, vbuf[slot, h].astype(BF16)))
        more = kt > 0
        go = jnp.where(more & (top > SB_DEAD), 1, 0)
        return go, kt - 1, 1 - slot, jnp.where(more, 1, 0)

    state = (jnp.where(top > SB_DEAD, 1, 0), jnp.int32(n_ct - 1), jnp.int32(0), jnp.int32(1))
    _, kt_left, slot_left, in_flight = lax.while_loop(cond, body, state)

    @pl.when(in_flight == 1)
    def _drain():
        for cp in tile_copies(kt_left, slot_left):
            cp.wait()

    for h in range(N_HEADS):
        o_ref[0, :, h * HEAD_DIM:(h + 1) * HEAD_DIM] = acc_ref[h].astype(o_ref.dtype)


def _sbc(qb16, k_new, v_new, k_cache, v_cache):
    bsz, t_new, _ = qb16.shape
    p_len = k_cache.shape[3]
    assert p_len % LANES == 0 and p_len >= LANES and t_new % 16 == 0 and t_new <= LANES
    q_map = lambda b: (b, 0, 0)
    return pl.pallas_call(
        functools.partial(_sbc_kernel, t_new=t_new, p_len=p_len),
        grid=(bsz,),
        in_specs=[pl.BlockSpec((1, t_new, D_HEADS), q_map),
                  pl.BlockSpec((1, t_new, D_HEADS), q_map),
                  pl.BlockSpec((1, t_new, D_HEADS), q_map),
                  pl.BlockSpec(memory_space=pl.ANY),
                  pl.BlockSpec(memory_space=pl.ANY)],
        out_specs=pl.BlockSpec((1, t_new, D_HEADS), q_map),
        out_shape=jax.ShapeDtypeStruct((bsz, t_new, D_HEADS), BF16),
        scratch_shapes=[pltpu.VMEM((2, N_HEADS, HEAD_DIM, LANES), F32),
                        pltpu.VMEM((2, N_HEADS, HEAD_DIM, LANES), F32),
                        pltpu.SemaphoreType.DMA((2, 2)),
                        pltpu.VMEM((N_HEADS, t_new, LANES), F32),
                        pltpu.VMEM((N_HEADS, t_new, HEAD_DIM), F32)],
        compiler_params=pltpu.CompilerParams(dimension_semantics=("arbitrary",),
                                             vmem_limit_bytes=VMEM_LIMIT),
        name="sbc",
    )(qb16, k_new, v_new, k_cache, v_cache)


def _post_kernel(oa_ref, ob_ref, sga_ref, sgb_ref, x_ref, wa_ref, wb_ref, wo_ref, g_ref,
                 wr_hi_ref, wr_lo_ref, br_ref, h_o, hn_o, comb_o, *, tm):
    ya = jnp.dot(oa_ref[...], wa_ref[...], preferred_element_type=F32)
    yb = jnp.dot(ob_ref[...], wb_ref[...], preferred_element_type=F32)
    mix = sga_ref[...] * ya + sgb_ref[...] * yb
    h = x_ref[...] + jnp.dot(mix.astype(BF16), wo_ref[...], preferred_element_type=F32)
    h_o[...] = h
    r = lax.rsqrt(jnp.mean(h * h, axis=-1, keepdims=True) + RMS_EPS)
    hn = (h * r) * g_ref[...]
    hn_hi = hn.astype(BF16)
    hn_o[...] = hn_hi
    hn_lo = (hn - hn_hi.astype(F32)).astype(BF16)
    logits = (jnp.dot(hn_hi, wr_hi_ref[...], preferred_element_type=F32)
              + jnp.dot(hn_lo, wr_hi_ref[...], preferred_element_type=F32)
              + jnp.dot(hn_hi, wr_lo_ref[...], preferred_element_type=F32)) + br_ref[...]

    lane = lax.broadcasted_iota(jnp.int32, (tm, LANES), 1)
    is_g = (lane >= N_EXPERTS) & (lane < N_EXPERTS + N_GROUPS)
    gl = jnp.where(is_g, logits, -jnp.inf)
    gmax = jnp.max(gl, axis=1, keepdims=True)
    g_lane = jnp.min(jnp.where(gl == gmax, lane, 2 ** 30), axis=1, keepdims=True)
    g_w = 1.0 / jnp.sum(jnp.exp(gl - gmax), axis=1, keepdims=True)
    in_grp = (lane < N_EXPERTS) & ((lane // EXPERTS_PER_GROUP) == (g_lane - N_EXPERTS))
    e1 = jnp.where(in_grp, logits, -jnp.inf)
    v1 = jnp.max(e1, axis=1, keepdims=True)
    i1 = jnp.min(jnp.where(e1 == v1, lane, 2 ** 30), axis=1, keepdims=True)
    e2 = jnp.where(lane == i1, -jnp.inf, e1)
    v2 = jnp.max(e2, axis=1, keepdims=True)
    i2 = jnp.min(jnp.where(e2 == v2, lane, 2 ** 30), axis=1, keepdims=True)
    t2 = jnp.exp(v2 - v1)
    den = 1.0 + t2
    comb_o[...] = jnp.where(lane == i1, (1.0 / den) * g_w,
                            jnp.where(lane == i2, (t2 / den) * g_w, 0.0))


def _post(oa, ob, sga, sgb, x2d, wa16, wb16, wo16, g_ffn, wr_hi, wr_lo, br):
    n = x2d.shape[0]
    tm = min(256, n)
    assert n % tm == 0
    row = lambda i: (i, 0)
    fix = lambda i: (0, 0)
    return pl.pallas_call(
        functools.partial(_post_kernel, tm=tm),
        grid=(n // tm,),
        in_specs=[pl.BlockSpec((tm, D_HEADS), row), pl.BlockSpec((tm, D_HEADS), row),
                  pl.BlockSpec((tm, D_MODEL), row), pl.BlockSpec((tm, D_MODEL), row),
                  pl.BlockSpec((tm, D_MODEL), row),
                  pl.BlockSpec((D_HEADS, D_MODEL), fix), pl.BlockSpec((D_HEADS, D_MODEL), fix),
                  pl.BlockSpec((D_MODEL, D_MODEL), fix), pl.BlockSpec((1, D_MODEL), fix),
                  pl.BlockSpec((D_MODEL, LANES), fix), pl.BlockSpec((D_MODEL, LANES), fix),
                  pl.BlockSpec((1, LANES), fix)],
        out_specs=[pl.BlockSpec((tm, D_MODEL), row), pl.BlockSpec((tm, D_MODEL), row),
                   pl.BlockSpec((tm, LANES), row)],
        out_shape=[jax.ShapeDtypeStruct((n, D_MODEL), F32),
                   jax.ShapeDtypeStruct((n, D_MODEL), BF16),
                   jax.ShapeDtypeStruct((n, LANES), F32)],
        compiler_params=pltpu.CompilerParams(dimension_semantics=("arbitrary",),
                                             vmem_limit_bytes=VMEM_LIMIT),
        name="post",
    )(oa, ob, sga, sgb, x2d, wa16, wb16, wo16, g_ffn, wr_hi, wr_lo, br)


MOE_ROWS = 256
ROW_ALIGN = 16


def _moe_kernel(hn_ref, comb_ref, h_ref, wg_ref, wu_ref, wd_ref, gf_ref, y_o,
                xs_ref, ws_ref, accs_ref, tri_ref, pos_ref, meta_ref, *, tm):
    e = pl.program_id(1)
    lane = lax.broadcasted_iota(jnp.int32, (tm, LANES), 1)

    @pl.when((pl.program_id(0) == 0) & (e == 0))
    def _constants():
        r = lax.broadcasted_iota(jnp.int32, (tm, tm), 0)
        c = lax.broadcasted_iota(jnp.int32, (tm, tm), 1)
        tri_ref[...] = jnp.where(r >= c, 1.0, 0.0).astype(BF16)

    @pl.when(e == 0)
    def _permute():
        comb = comb_ref[...]
        used = jnp.where(comb != 0.0, 1.0, 0.0)
        in_group = jnp.zeros((tm, LANES), F32)
        rest = jnp.zeros((tm, 1), F32)
        for g in range(N_GROUPS - 1, 0, -1):
            sel = (lane >= g * EXPERTS_PER_GROUP) & (lane < (g + 1) * EXPERTS_PER_GROUP)
            m_g = jnp.max(jnp.where(sel, used, 0.0), axis=1, keepdims=True) * (1.0 - rest)
            in_group = jnp.where(lane == g, m_g, in_group)
            rest = rest + m_g
        in_group = jnp.where(lane == 0, 1.0 - rest, in_group)
        rank = jnp.dot(tri_ref[...], in_group.astype(BF16), preferred_element_type=F32)
        total = rank[tm - 1:tm, :]
        counts = [jnp.sum(jnp.where(lane[:1] == g, total, 0.0)).astype(jnp.int32)
                  for g in range(N_GROUPS)]
        start = jnp.int32(0)
        start_vec = jnp.zeros((1, LANES), F32)
        for g in range(N_GROUPS):
            meta_ref[g] = start
            meta_ref[N_GROUPS + g] = counts[g]
            start_vec = jnp.where(lane[:1] == g, start.astype(F32), start_vec)
            start = start + counts[g]
        pos = jnp.sum(in_group * (start_vec + rank - 1.0), axis=1, keepdims=True)
        pos_b = jnp.broadcast_to(pos, (tm, LANES))
        pos_ref[...] = pos_b
        pos_row = pos_b.T[0:1, :]
        slot = lax.broadcasted_iota(jnp.int32, (tm, tm), 0).astype(F32)
        perm = jnp.where(pos_row == slot, 1.0, 0.0).astype(BF16)
        xs_ref[0:tm, :] = jnp.dot(perm, hn_ref[...], preferred_element_type=F32).astype(BF16)
        c_hi = comb.astype(BF16)
        r1 = comb - c_hi.astype(F32)
        c_mid = r1.astype(BF16)
        c_lo = (r1 - c_mid.astype(F32)).astype(BF16)
        ws_ref[0:tm, :] = (jnp.dot(perm, c_hi, preferred_element_type=F32)
                           + jnp.dot(perm, c_mid, preferred_element_type=F32)
                           + jnp.dot(perm, c_lo, preferred_element_type=F32))
        xs_ref[tm:, :] = jnp.zeros((MOE_ROWS, D_MODEL), BF16)
        ws_ref[tm:, :] = jnp.zeros((MOE_ROWS, LANES), F32)
        accs_ref[...] = jnp.zeros(accs_ref.shape, F32)

    g = e // EXPERTS_PER_GROUP
    start = meta_ref[g]
    count = meta_ref[N_GROUPS + g]
    first = (start // ROW_ALIGN) * ROW_ALIGN
    n_blocks = (start + count - first + MOE_ROWS - 1) // MOE_ROWS
    lane_b = lax.broadcasted_iota(jnp.int32, (MOE_ROWS, LANES), 1)

    def block(j, carry):
        rows = pl.ds(pl.multiple_of(first + j * MOE_ROWS, ROW_ALIGN), MOE_ROWS)
        x = xs_ref[rows, :]
        w = jnp.sum(jnp.where(lane_b == e, ws_ref[rows, :], 0.0), axis=1, keepdims=True)
        gate = jnp.dot(x, wg_ref[0].astype(BF16), preferred_element_type=F32)
        up = jnp.dot(x, wu_ref[0].astype(BF16), preferred_element_type=F32)
        hid = (gate * jax.nn.sigmoid(gate)) * up * w
        accs_ref[rows, :] += jnp.dot(hid.astype(BF16), wd_ref[0].astype(BF16),
                                     preferred_element_type=F32)
        return carry
    lax.fori_loop(0, jnp.where(count > 0, n_blocks, 0), block, 0)

    @pl.when(e == N_EXPERTS - 1)
    def _finish():
        slot = lax.broadcasted_iota(jnp.int32, (tm, tm), 1).astype(F32)
        back = jnp.where(pos_ref[:, 0:1] == slot, 1.0, 0.0).astype(BF16)
        ffn = jnp.dot(back, accs_ref[0:tm, :].astype(BF16), preferred_element_type=F32)
        out = h_ref[...] + ffn
        r = lax.rsqrt(jnp.mean(out * out, axis=-1, keepdims=True) + RMS_EPS)
        y_o[...] = (out * r) * gf_ref[...]


def _moe(hn16, comb, h, wg, wu, wd, g_final):
    n = h.shape[0]
    tm = min(1024, n)
    assert n % tm == 0 and tm % LANES == 0
    row = lambda i, e: (i, 0)
    exp = lambda i, e: (e, 0, 0)
    fix = lambda i, e: (0, 0)
    return pl.pallas_call(
        functools.partial(_moe_kernel, tm=tm),
        grid=(n // tm, N_EXPERTS),
        in_specs=[pl.BlockSpec((tm, D_MODEL), row), pl.BlockSpec((tm, LANES), row),
                  pl.BlockSpec((tm, D_MODEL), row),
                  pl.BlockSpec((1, D_MODEL, D_FF_EXPERT), exp),
                  pl.BlockSpec((1, D_MODEL, D_FF_EXPERT), exp),
                  pl.BlockSpec((1, D_FF_EXPERT, D_MODEL), exp),
                  pl.BlockSpec((1, D_MODEL), fix)],
        out_specs=pl.BlockSpec((tm, D_MODEL), row),
        out_shape=jax.ShapeDtypeStruct((n, D_MODEL), F32),
        scratch_shapes=[pltpu.VMEM((tm + MOE_ROWS, D_MODEL), BF16),
                        pltpu.VMEM((tm + MOE_ROWS, LANES), F32),
                        pltpu.VMEM((tm + MOE_ROWS, D_MODEL), F32),
                        pltpu.VMEM((tm, tm), BF16),
                        pltpu.VMEM((tm, LANES), F32),
                        pltpu.SMEM((2 * N_GROUPS,), jnp.int32)],
        compiler_params=pltpu.CompilerParams(dimension_semantics=("arbitrary", "arbitrary"),
                                             vmem_limit_bytes=VMEM_LIMIT),
        name="moe",
    )(hn16, comb, h, wg, wu, wd, g_final)


def _prep_weights(norm_mix_g, w_in, w_br_a, w_br_b, w_out, norm_ffn_g, w_rg, b_rg, w_re, b_re,
                  w_eg, w_eu, w_ed, norm_final_g):
    offs = np.cumsum(IN_SPLIT)[:-1].tolist()
    q_a, k_a, v_a, q_i, k_i, w_i, q_b, k_b, v_b, g_a, g_b = jnp.split(w_in, offs, axis=-1)
    w_i_pad = jnp.pad(w_i, ((0, 0), (0, LANES - N_IDX_HEADS)))
    w16 = jnp.concatenate([q_a, k_a, v_a, q_i, k_i, k_i, w_i_pad, q_b, k_b, v_b, g_a, g_b],
                          axis=1).astype(BF16)
    half = ROT_HALF
    inv_freq = ROPE_THETA ** (-jnp.arange(half, dtype=F32) / half)
    d = np.arange(LANES) % HEAD_DIM
    invf = jnp.where(jnp.asarray(d < ROT_DIM), inv_freq[jnp.asarray(d % half)], 0.0)[None, :]
    pad = LANES - N_EXPERTS - N_GROUPS
    w_r = jnp.pad(jnp.concatenate([w_re, w_rg], axis=1), ((0, 0), (0, pad)))
    wr_hi = w_r.astype(BF16)
    wr_lo = (w_r - wr_hi.astype(F32)).astype(BF16)
    b_r = jnp.pad(jnp.concatenate([b_re, b_rg]), (0, pad))[None, :]
    return dict(g_mix=norm_mix_g[None, :], w16=w16, invf=invf,
                wa16=w_br_a.astype(BF16), wb16=w_br_b.astype(BF16), wo16=w_out.astype(BF16),
                g_ffn=norm_ffn_g[None, :], wr_hi=wr_hi, wr_lo=wr_lo, b_r=b_r,
                wg=w_eg, wu=w_eu, wd=w_ed,
                g_final=norm_final_g[None, :])


def _layer(x, past, wts, *, dsa_qb, dsa_kt, sb_qb):
    bsz, t, _ = x.shape
    n = bsz * t
    x2d = x.reshape(n, D_MODEL)
    pos0 = 0 if past is None else past[0].shape[1]
    qa_scale = ATT_SCALE * LOG2_E if past is None else ATT_SCALE
    pr = _proj(x2d, wts["g_mix"], wts["w16"], wts["invf"], tq=t, pos0=pos0, qa_scale=qa_scale)
    shp = lambda a: a.reshape(bsz, t, a.shape[-1])

    if past is None:
        vt = pr["va16"].reshape(bsz, t, N_HEADS, HEAD_DIM).transpose(0, 2, 3, 1)
        vt = jnp.concatenate([vt, jnp.ones((bsz, N_HEADS, 1, t), BF16),
                              jnp.zeros((bsz, N_HEADS, LANES - HEAD_DIM - 1, t), BF16)], axis=2)
        oa = _dsat(shp(pr["qa"]), shp(pr["qi"]), shp(pr["wi"]), shp(pr["ki16"]), shp(pr["ka16"]),
                   vt.reshape(bsz, N_HEADS * LANES, t), pos0=0, l_valid=t,
                   qb_rows=dsa_qb, kt_rows=dsa_kt)
        ob = _sb(shp(pr["qb"]), shp(pr["kb16"]), shp(pr["vb16"]), pos0=0, qb_rows=sb_qb)
    else:
        keys_last = lambda c: jnp.transpose(c, (0, 2, 3, 1))
        oa = _dsac(shp(pr["qa"]), shp(pr["qi"]), shp(pr["wi"]), jnp.swapaxes(past[2], 1, 2),
                   shp(pr["ki16"]), keys_last(past[0]), keys_last(past[1]),
                   shp(pr["ka16"]), shp(pr["va16"]), kt_rows=dsa_kt)
        ob = _sbc(shp(pr["qb"]), shp(pr["kb16"]), shp(pr["vb16"]),
                  keys_last(past[3]), keys_last(past[4]))
    h, hn16, comb = _post(oa.reshape(n, D_HEADS), ob.reshape(n, D_HEADS), pr["sga"], pr["sgb"],
                          x2d, wts["wa16"], wts["wb16"], wts["wo16"], wts["g_ffn"],
                          wts["wr_hi"], wts["wr_lo"], wts["b_r"])
    y = _moe(hn16, comb, h, wts["wg"], wts["wu"], wts["wd"], wts["g_final"])
    heads = lambda a: a.reshape(1, bsz, t, N_HEADS, HEAD_DIM)
    rows = (heads(pr["ka"]), heads(pr["va"]),
            pr["ki"][:, :IDX_DIM].reshape(1, bsz, t, IDX_DIM),
            heads(pr["kb"]), heads(pr["vb"]))
    return y.reshape(bsz, t, D_MODEL), rows


def kernel(x_prompt, x_sample, cache_a_k, cache_a_v, cache_idx_k, cache_b_k, cache_b_v,
           norm_mix_g, w_in, w_br_a, w_br_b, w_out, norm_ffn_g,
           w_router_group, b_router_group, w_router_expert, b_router_expert,
           w_exp_gate, w_exp_up, w_exp_down, norm_final_g):
    assert w_in.shape[0] == 1, "single-layer model"
    wts = _prep_weights(norm_mix_g[0], w_in[0], w_br_a[0], w_br_b[0], w_out[0], norm_ffn_g[0],
                        w_router_group[0], b_router_group[0], w_router_expert[0],
                        b_router_expert[0], w_exp_gate[0], w_exp_up[0], w_exp_down[0],
                        norm_final_g)
    y_p, rows_p = _layer(x_prompt, None, wts, dsa_qb=256, dsa_kt=1024, sb_qb=128)
    past = (cache_a_k[0], cache_a_v[0], cache_idx_k[0], cache_b_k[0], cache_b_v[0])
    y_s, rows_s = _layer(x_sample, past, wts, dsa_qb=None, dsa_kt=1024, sb_qb=None)
    return (y_p, y_s) + rows_p + rows_s
```

```python
import functools

import numpy as np
import jax
import jax.numpy as jnp
from jax import lax
from jax.experimental import pallas as pl
from jax.experimental.pallas import tpu as pltpu

D_MODEL = 1024
HEAD_DIM = 64
N_HEADS = 8
D_HEADS = N_HEADS * HEAD_DIM
N_IDX_HEADS = 8
IDX_DIM = 64
IDX_SCALE = (N_IDX_HEADS * IDX_DIM) ** -0.5
ATT_SCALE = HEAD_DIM ** -0.5
LOG2_E = 1.4426950408889634
CHUNK = 64
TOPK_MAX = 256
ROPE_THETA = 500000.0
ROT_DIM = HEAD_DIM // 4
ROT_HALF = ROT_DIM // 2
N_GROUPS = 4
EXPERTS_PER_GROUP = 8
N_EXPERTS = N_GROUPS * EXPERTS_PER_GROUP
D_FF_EXPERT = D_MODEL // 4
RMS_EPS = 1e-6
IN_SPLIT = (D_HEADS, D_HEADS, D_HEADS, N_IDX_HEADS * IDX_DIM, IDX_DIM, N_IDX_HEADS,
            D_HEADS, D_HEADS, D_HEADS, D_MODEL, D_MODEL)

LANES = 128
SUBLANES = 8
N_PAIRS = N_HEADS // 2
HEADS_PER_STAGE = 8
NEG_BIG = -1e30
SB_DEAD = -110.0
VMEM_LIMIT = 56 * 1024 * 1024

C_QA, C_KA, C_VA, C_QI, C_KI, C_WI, C_QB, C_KB, C_VB, C_GA, C_GB, C_END = (
    0, 512, 1024, 1536, 2048, 2176, 2304, 2816, 3328, 3840, 4864, 5888)

F32 = jnp.float32
BF16 = jnp.bfloat16


def _nt_dot(a, b):
    return lax.dot_general(a, b, (((1,), (1,)), ((), ())), preferred_element_type=F32)


def _reduce0(x, op, ways=8):
    n = x.shape[0]
    if n % ways != 0 or n <= ways:
        ways = 1
    part = x.reshape(n // ways, ways, *x.shape[1:])
    acc = part[0]
    for j in range(1, n // ways):
        acc = op(acc, part[j])
    out = acc[0]
    for j in range(1, ways):
        out = op(out, acc[j])
    return out


def _head_halves(x):
    lane = lax.broadcasted_iota(jnp.int32, x.shape, 1)
    zero = jnp.zeros_like(x)
    return jnp.where(lane < HEAD_DIM, x, zero), jnp.where(lane >= HEAD_DIM, x, zero)


def _proj_kernel(x_ref, g_ref, w_ref, invf_ref,
                 qa_o, qi_o, qb_o, ka_o, va_o, kb_o, vb_o, ki_o, wi_o,
                 ka16_o, va16_o, kb16_o, vb16_o, ki16_o, sga_o, sgb_o,
                 *, tm, tq, pos0, qa_scale):
    x = x_ref[...]
    r = lax.rsqrt(jnp.mean(x * x, axis=-1, keepdims=True) + RMS_EPS)
    xn = (x * r) * g_ref[...]
    p = jnp.dot(xn.astype(BF16), w_ref[...], preferred_element_type=F32)

    row = pl.program_id(0) * tm + lax.broadcasted_iota(jnp.int32, (tm, 1), 0)
    pos = (row % tq + pos0).astype(F32)
    ang = pos * invf_ref[...]
    c = jnp.cos(ang)
    s = jnp.sin(ang)
    d = lax.broadcasted_iota(jnp.int32, (1, LANES), 1) % HEAD_DIM
    s_lo = jnp.where(d < ROT_HALF, -s, 0.0)
    s_hi = jnp.where((d >= ROT_HALF) & (d < ROT_DIM), s, 0.0)

    def rope(v):
        return (v * c + pltpu.roll(v, LANES - ROT_HALF, 1) * s_lo
                + pltpu.roll(v, ROT_HALF, 1) * s_hi)

    for j in range(D_HEADS // LANES):
        sl = slice(j * LANES, (j + 1) * LANES)
        qa = rope(p[:, C_QA + j * LANES:C_QA + (j + 1) * LANES])
        qa_o[:, sl] = (qa * qa_scale).astype(BF16)
        ka = rope(p[:, C_KA + j * LANES:C_KA + (j + 1) * LANES])
        ka_o[:, sl] = ka
        ka16_o[:, sl] = ka.astype(BF16)
        qi = rope(p[:, C_QI + j * LANES:C_QI + (j + 1) * LANES])
        qi_o[:, sl] = qi.astype(BF16)
    ki = rope(p[:, C_KI:C_KI + LANES])
    ki_o[...] = ki
    ki16_o[...] = ki.astype(BF16)
    wi_o[...] = p[:, C_WI:C_WI + LANES]
    va = p[:, C_VA:C_VA + D_HEADS]
    va_o[...] = va
    va16_o[...] = va.astype(BF16)
    qb_o[...] = (p[:, C_QB:C_QB + D_HEADS] * ATT_SCALE).astype(BF16)
    kb = p[:, C_KB:C_KB + D_HEADS]
    kb_o[...] = kb
    kb16_o[...] = kb.astype(BF16)
    vb = p[:, C_VB:C_VB + D_HEADS]
    vb_o[...] = vb
    vb16_o[...] = vb.astype(BF16)
    sga_o[...] = jax.nn.sigmoid(p[:, C_GA:C_GA + D_MODEL])
    sgb_o[...] = jax.nn.sigmoid(p[:, C_GB:C_GB + D_MODEL])


def _proj(x2d, g, w16, invf, *, tq, pos0, qa_scale):
    n = x2d.shape[0]
    tm = min(256, n)
    assert n % tm == 0
    row = lambda i: (i, 0)
    fix = lambda i: (0, 0)
    widths = dict(qa=D_HEADS, qi=D_HEADS, qb=D_HEADS, ka=D_HEADS, va=D_HEADS, kb=D_HEADS,
                  vb=D_HEADS, ki=LANES, wi=LANES, ka16=D_HEADS, va16=D_HEADS, kb16=D_HEADS,
                  vb16=D_HEADS, ki16=LANES, sga=D_MODEL, sgb=D_MODEL)
    dtypes = dict(qa=BF16, qi=BF16, qb=BF16, ka16=BF16, va16=BF16, kb16=BF16, vb16=BF16, ki16=BF16)
    names = list(widths)
    out_shape = [jax.ShapeDtypeStruct((n, widths[k]), dtypes.get(k, F32)) for k in names]
    out_specs = [pl.BlockSpec((tm, widths[k]), row) for k in names]
    outs = pl.pallas_call(
        functools.partial(_proj_kernel, tm=tm, tq=tq, pos0=pos0, qa_scale=qa_scale),
        grid=(n // tm,),
        in_specs=[pl.BlockSpec((tm, D_MODEL), row),
                  pl.BlockSpec((1, D_MODEL), fix),
                  pl.BlockSpec((D_MODEL, C_END), fix),
                  pl.BlockSpec((1, LANES), fix)],
        out_specs=out_specs,
        out_shape=out_shape,
        compiler_params=pltpu.CompilerParams(dimension_semantics=("arbitrary",),
                                             vmem_limit_bytes=VMEM_LIMIT),
        name="proj",
    )(x2d, g, w16, invf)
    return dict(zip(names, outs))


def _dsac_kernel(qa_ref, qi_ref, wi_ref, kic_ref, kin_ref, kc_ref, vc_ref, kn_ref, vn_ref, o_ref,
                 score_ref, thr_ref, jcut_ref, m_ref, l_ref, acc_ref,
                 *, n_ct, kt_rows, t_new, p_len, topk):
    kt = pl.program_id(1)
    qb_rows = t_new
    ch = kt_rows
    n_ch = n_ct + 1
    l_valid = p_len + t_new

    row = lax.broadcasted_iota(jnp.int32, (qb_rows, 1), 0)
    limit = jnp.minimum(((p_len + row) // CHUNK + 1) * CHUNK, l_valid)

    def pad_keys(x):
        return jnp.concatenate([x, jnp.zeros((LANES - t_new, HEAD_DIM), x.dtype)], axis=0)

    def count_where(pred_fn):
        def body(t, acc):
            blk = score_ref[t]
            for c in range(ch // LANES):
                acc = acc + jnp.where(pred_fn(t, blk[:, c * LANES:(c + 1) * LANES], c), 1.0, 0.0)
            return acc
        acc = lax.fori_loop(0, n_ch, body, jnp.zeros((qb_rows, LANES), F32))
        return jnp.sum(acc, axis=1, keepdims=True)

    @pl.when(kt == 0)
    def _select():
        qi = qi_ref[0]
        q_idx = [qi[:, h * IDX_DIM:(h + 1) * IDX_DIM] for h in range(N_IDX_HEADS)]
        w = wi_ref[0] * IDX_SCALE
        w_cols = [w[:, h:h + 1] for h in range(N_IDX_HEADS)]

        def scores(rel_of_head, n_keys):
            s = jnp.zeros((qb_rows, n_keys), F32)
            for h in range(N_IDX_HEADS):
                s = s + w_cols[h] * jnp.maximum(rel_of_head(h), 0.0)
            return s

        for t in range(n_ct):
            kk_t = kic_ref[0, :, t * ch:(t + 1) * ch].astype(BF16)
            kpos = t * ch + lax.broadcasted_iota(jnp.int32, (1, ch), 1)
            s = scores(lambda h: jnp.dot(q_idx[h], kk_t, preferred_element_type=F32), ch)
            score_ref[t] = jnp.where(kpos < limit, s, -jnp.inf)
        kk_new = pad_keys(kin_ref[0][:, :IDX_DIM])
        s_new = scores(lambda h: _nt_dot(q_idx[h], kk_new), LANES)
        kpos = p_len + lax.broadcasted_iota(jnp.int32, (1, LANES), 1)
        score_ref[n_ct] = jnp.concatenate(
            [jnp.where(kpos < limit, s_new, -jnp.inf),
             jnp.full((qb_rows, ch - LANES), -jnp.inf, F32)], axis=1)

        def minmax_body(t, carry):
            mn, mx = carry
            blk = score_ref[t]
            for c in range(ch // LANES):
                piece = blk[:, c * LANES:(c + 1) * LANES]
                mx = jnp.maximum(mx, piece)
                mn = jnp.minimum(mn, jnp.where(piece == -jnp.inf, jnp.inf, piece))
            return mn, mx
        mn, mx = lax.fori_loop(0, n_ch, minmax_body,
                               (jnp.full((qb_rows, LANES), jnp.inf, F32),
                                jnp.full((qb_rows, LANES), -jnp.inf, F32)))
        lo0 = jnp.min(mn, axis=1, keepdims=True)
        smax = jnp.max(mx, axis=1, keepdims=True)
        hi0 = smax + (jnp.abs(smax) * 1e-6 + 1e-30)
        all_in = limit <= topk
        done0 = jnp.where(all_in, 1.0, 0.0)
        thr0 = jnp.where(all_in, jnp.float32(-3e38), lo0)
        zeros = jnp.zeros((qb_rows, 1), F32)

        def cond(c):
            return c[0] > 0

        def bisect_pass(lo, hi, chi, thr, done, tie, cand, pend):
            half = lo + 0.5 * (hi - lo)
            has_cand = pend > 0.5
            mid = jnp.where(has_cand, cand, half)
            stuck = jnp.logical_not(has_cand) & ((half <= lo) | (half >= hi))
            mid_b = jnp.broadcast_to(mid, (qb_rows, LANES))
            cnt = count_where(lambda t, piece, cidx: piece >= mid_b)
            active = done < 0.5
            moving = active & jnp.logical_not(stuck)
            found = moving & ((cnt == topk) | (has_cand & (cnt > topk)))
            now_tie = (active & stuck) | (moving & has_cand & (cnt > topk))
            up = moving & jnp.logical_not(has_cand) & (cnt > topk)
            down = moving & (cnt < topk)
            thr = jnp.where(found, mid, jnp.where(active & stuck, lo, thr))
            tie = jnp.where(now_tie, 1.0, tie)
            done = jnp.where(found | now_tie, 1.0, done)
            lo = jnp.where(up, mid, lo)
            chi = jnp.where(down, cnt, chi)
            hi = jnp.where(down, mid, hi)
            return lo, hi, chi, thr, done, tie, cand, jnp.zeros_like(pend)

        def snap_pass(lo, hi, chi, thr, done, tie, cand, pend):
            hi_b = jnp.broadcast_to(hi, (qb_rows, LANES))

            def body(t, acc):
                blk = score_ref[t]
                for c in range(ch // LANES):
                    piece = blk[:, c * LANES:(c + 1) * LANES]
                    acc = jnp.maximum(acc, jnp.where(piece < hi_b, piece, -jnp.inf))
                return acc
            acc = lax.fori_loop(0, n_ch, body, jnp.full((qb_rows, LANES), -jnp.inf, F32))
            cand = jnp.max(acc, axis=1, keepdims=True)
            return lo, hi, chi, thr, done, tie, cand, 1.0 - done

        def body(c):
            it = c[1]
            snap = (it >= 11) & (it % 4 == 3)
            new = lax.cond(snap, snap_pass, bisect_pass, *c[2:])
            left = jnp.sum(1.0 - new[4])
            go = jnp.where((left > 0) & (it < 400), 1, 0)
            return (go, it + 1) + tuple(new)

        left0 = jnp.sum(1.0 - done0)
        state = (jnp.where(left0 > 0, 1, 0), jnp.int32(0), lo0, hi0, zeros, thr0, done0, zeros,
                 zeros, zeros)
        _, _, _, _, chi, thr, _, tie, _, _ = lax.while_loop(cond, body, state)

        big = jnp.full((qb_rows, 1), 2 ** 30, jnp.int32)

        def tie_break():
            need = topk - chi
            thr_b = jnp.broadcast_to(thr, (qb_rows, LANES))
            lane1 = lax.broadcasted_iota(jnp.int32, (1, LANES), 1)

            def tb_body(_, c):
                lo_j, hi_j = c
                mid_j = lo_j + (hi_j - lo_j) // 2
                mid_jb = jnp.broadcast_to(mid_j, (qb_rows, LANES))
                cnt = count_where(lambda t, piece, cidx: (piece == thr_b)
                                  & (t * ch + cidx * LANES + lane1 <= mid_jb))
                ok = cnt >= need
                return jnp.where(ok, lo_j, mid_j), jnp.where(ok, mid_j, hi_j)

            lo_j = jnp.full((qb_rows, 1), -1, jnp.int32)
            hi_j = jnp.zeros((qb_rows, 1), jnp.int32) + (n_ch * ch - 1)
            n_steps = int(np.ceil(np.log2(score_ref.shape[0] * ch))) + 1
            _, hi_j = lax.fori_loop(0, n_steps, tb_body, (lo_j, hi_j))
            return jnp.where(tie > 0, hi_j, big)

        jcut = lax.cond(jnp.sum(tie) > 0, tie_break, lambda: big)
        thr_ref[...] = jnp.broadcast_to(thr, (qb_rows, LANES))
        jcut_ref[...] = jnp.broadcast_to(jcut, (qb_rows, LANES))
        m_ref[...] = jnp.full(m_ref.shape, NEG_BIG, F32)
        l_ref[...] = jnp.zeros(l_ref.shape, F32)
        acc_ref[...] = jnp.zeros(acc_ref.shape, F32)

    def update(n_keys, logits_of_head, pv_of_head):
        thr_b = thr_ref[...]
        jcut_b = jcut_ref[...]
        lane1 = lax.broadcasted_iota(jnp.int32, (1, LANES), 1)
        blk = score_ref[kt]
        bias = []
        for c in range(n_keys // LANES):
            piece = blk[:, c * LANES:(c + 1) * LANES]
            kpos = kt * kt_rows + c * LANES + lane1
            sel = (piece >= thr_b) & ((piece > thr_b) | (kpos <= jcut_b))
            bias.append(jnp.where(sel, 0.0, NEG_BIG))
        n_piece = len(bias)
        qa = qa_ref[0]
        heads = range(N_HEADS)

        def fold(xs, op):
            out = xs[0]
            for x in xs[1:]:
                out = op(out, x)
            return out

        logits = [logits_of_head(h, qa[:, h * HEAD_DIM:(h + 1) * HEAD_DIM]) for h in heads]
        pieces = [[logits[h][:, c * LANES:(c + 1) * LANES] + bias[c] for c in range(n_piece)]
                  for h in heads]
        m_old = [m_ref[h] for h in heads]
        m_new = [jnp.maximum(m_old[h], jnp.max(fold(pieces[h], jnp.maximum), axis=1, keepdims=True))
                 for h in heads]
        alpha = [jnp.exp(m_old[h] - m_new[h]) for h in heads]
        ps = [[jnp.exp(pc - m_new[h]) for pc in pieces[h]] for h in heads]
        pv = [pv_of_head(h, jnp.concatenate([pc.astype(BF16) for pc in ps[h]], axis=1))
              for h in heads]
        for h in heads:
            m_ref[h] = m_new[h]
            l_ref[h] = alpha[h] * l_ref[h] + fold(ps[h], jnp.add)
            acc_ref[h] = alpha[h][:, :HEAD_DIM] * acc_ref[h] + pv[h]

    @pl.when(kt < n_ct)
    def _cache_tile():
        update(kt_rows,
               lambda h, q_h: jnp.dot(q_h, kc_ref[0, h].astype(BF16), preferred_element_type=F32),
               lambda h, p16: _nt_dot(p16, vc_ref[0, h].astype(BF16)))

    @pl.when(kt == n_ct)
    def _new_tile():
        kn = kn_ref[0]
        vn = vn_ref[0]
        head = lambda x, h: pad_keys(x[:, h * HEAD_DIM:(h + 1) * HEAD_DIM])
        update(LANES,
               lambda h, q_h: _nt_dot(q_h, head(kn, h)),
               lambda h, p16: jnp.dot(p16, head(vn, h), preferred_element_type=F32))
        for h in range(N_HEADS):
            denom = jnp.sum(l_ref[h], axis=1, keepdims=True)
            o_ref[0, :, h * HEAD_DIM:(h + 1) * HEAD_DIM] = (acc_ref[h] / denom).astype(o_ref.dtype)


def _dsat_kernel(qb_of, kt_of, first_of, last_of,
                 qa_ref, qi_ref, wi_ref, ki_ref, k_ref, vt_ref, o_ref,
                 score_ref, m_ref, acc_ref,
                 *, qb_rows, kt_rows, pos0, l_valid, topk):
    i = pl.program_id(1)
    qb = qb_of[i]
    kt = kt_of[i]
    nq = qb_rows
    ns = kt_rows // SUBLANES

    qcol = qb * nq + lax.broadcasted_iota(jnp.int32, (SUBLANES, nq), 1)
    limit = jnp.minimum(((pos0 + qcol) // CHUNK + 1) * CHUNK, l_valid)
    max_limit = jnp.minimum(((pos0 + qb * nq + nq - 1) // CHUNK + 1) * CHUNK, l_valid)
    n_kt = (max_limit + kt_rows - 1) // kt_rows
    kidx = (lax.broadcasted_iota(jnp.int32, (ns, SUBLANES, nq), 0) * SUBLANES
            + lax.broadcasted_iota(jnp.int32, (ns, SUBLANES, nq), 1))

    def reduce_tiles(fn, op, init):
        def body(t, acc):
            return op(acc, fn(t, score_ref[t]))
        return lax.fori_loop(0, n_kt, body, init)

    def count_where(pred):
        acc = reduce_tiles(lambda t, blk: _reduce0(jnp.where(pred(t, blk), 1.0, 0.0), jnp.add),
                           jnp.add, jnp.zeros((SUBLANES, nq), F32))
        return jnp.broadcast_to(jnp.sum(acc, axis=0, keepdims=True), (SUBLANES, nq))

    @pl.when(first_of[i] == 1)
    def _select():
        qi = qi_ref[0]
        q_halves = []
        for g in range(N_PAIRS):
            q_halves.extend(_head_halves(qi[:, g * LANES:(g + 1) * LANES]))
        w_t = wi_ref[0].T * IDX_SCALE
        w_rows = [jnp.broadcast_to(w_t[h:h + 1, :], (SUBLANES, nq)) for h in range(N_IDX_HEADS)]

        def score_body(t, carry):
            mn, mx, n_pos, n_nonneg = carry
            kk = ki_ref[0, pl.ds(pl.multiple_of(t * kt_rows, kt_rows), kt_rows), :]
            s = jnp.zeros((ns, SUBLANES, nq), F32)
            for h in range(N_IDX_HEADS):
                rel = jnp.maximum(_nt_dot(kk, q_halves[h]), 0.0)
                s = s + w_rows[h] * rel.reshape(ns, SUBLANES, nq)
            seen = kidx + t * kt_rows < limit
            s_lo = jnp.where(seen, s, -jnp.inf)
            score_ref[t] = s_lo
            mn = jnp.minimum(mn, _reduce0(jnp.where(seen, s, jnp.inf), jnp.minimum))
            mx = jnp.maximum(mx, _reduce0(s_lo, jnp.maximum))
            n_pos = n_pos + _reduce0(jnp.where(s_lo > 0.0, 1.0, 0.0), jnp.add)
            n_nonneg = n_nonneg + _reduce0(jnp.where(s_lo >= 0.0, 1.0, 0.0), jnp.add)
            return mn, mx, n_pos, n_nonneg
        zeros = jnp.zeros((SUBLANES, nq), F32)
        mn8, mx8, n_pos8, n_nonneg8 = lax.fori_loop(
            0, n_kt, score_body,
            (jnp.full((SUBLANES, nq), jnp.inf, F32), jnp.full((SUBLANES, nq), -jnp.inf, F32),
             zeros, zeros))

        rep = lambda v: jnp.broadcast_to(v, (SUBLANES, nq))
        lo0 = rep(jnp.min(mn8, axis=0, keepdims=True))
        smax = rep(jnp.max(mx8, axis=0, keepdims=True))
        hi0 = smax + (jnp.abs(smax) * 1e-6 + 1e-30)
        n_pos = rep(jnp.sum(n_pos8, axis=0, keepdims=True))
        n_nonneg = rep(jnp.sum(n_nonneg8, axis=0, keepdims=True))
        all_in = limit <= topk
        below0 = jnp.logical_not(all_in) & (n_nonneg < topk)
        at0 = jnp.logical_not(all_in) & (n_pos < topk) & (n_nonneg >= topk)
        above0 = jnp.logical_not(all_in) & (n_pos >= topk)
        hi0 = jnp.where(below0, 0.0, hi0)
        chi0 = jnp.where(below0, n_nonneg, jnp.where(at0, n_pos, 0.0))
        lo0 = jnp.where(above0, 0.0, lo0)
        done0 = jnp.where(all_in | at0, 1.0, 0.0)
        tie0 = jnp.where(at0 & (n_nonneg > topk), 1.0, 0.0)
        thr0 = jnp.where(all_in, jnp.float32(-3e38), jnp.where(at0, 0.0, lo0))

        def bisect_pass(lo, hi, chi, thr, done, tie, cand, pend):
            half = lo + 0.5 * (hi - lo)
            has_cand = pend > 0.5
            mid = jnp.where(has_cand, cand, half)
            stuck = jnp.logical_not(has_cand) & ((half <= lo) | (half >= hi))
            cnt = count_where(lambda t, blk: blk >= mid)
            active = done < 0.5
            moving = active & jnp.logical_not(stuck)
            found = moving & ((cnt == topk) | (has_cand & (cnt > topk)))
            now_tie = (active & stuck) | (moving & has_cand & (cnt > topk))
            up = moving & jnp.logical_not(has_cand) & (cnt > topk)
            down = moving & (cnt < topk)
            thr = jnp.where(found, mid, jnp.where(active & stuck, lo, thr))
            tie = jnp.where(now_tie, 1.0, tie)
            done = jnp.where(found | now_tie, 1.0, done)
            lo = jnp.where(up, mid, lo)
            chi = jnp.where(down, cnt, chi)
            hi = jnp.where(down, mid, hi)
            return lo, hi, chi, thr, done, tie, cand, jnp.zeros_like(pend)

        def snap_pass(lo, hi, chi, thr, done, tie, cand, pend):
            below = reduce_tiles(
                lambda t, blk: _reduce0(jnp.where(blk < hi, blk, -jnp.inf), jnp.maximum),
                jnp.maximum, jnp.full((SUBLANES, nq), -jnp.inf, F32))
            cand = rep(jnp.max(below, axis=0, keepdims=True))
            return lo, hi, chi, thr, done, tie, cand, 1.0 - done

        def cond(c):
            return c[0] > 0

        def body(c):
            it = c[1]
            snap = (it >= 11) & (it % 4 == 3)
            new = lax.cond(snap, snap_pass, bisect_pass, *c[2:])
            left = jnp.sum(1.0 - new[4])
            go = jnp.where((left > 0) & (it < 400), 1, 0)
            return (go, it + 1) + tuple(new)

        left0 = jnp.sum(1.0 - done0)
        state = (jnp.where(left0 > 0, 1, 0), jnp.int32(0), lo0, hi0, chi0, thr0, done0, tie0,
                 zeros, zeros)
        _, _, _, _, chi, thr, _, tie, _, _ = lax.while_loop(cond, body, state)

        big = jnp.full((SUBLANES, nq), 2 ** 30, jnp.int32)

        def tie_break():
            need = topk - chi

            def tb_body(_, c):
                lo_j, hi_j = c
                mid_j = lo_j + (hi_j - lo_j) // 2
                cnt = count_where(lambda t, blk: (blk == thr) & (kidx + t * kt_rows <= mid_j))
                ok = cnt >= need
                return jnp.where(ok, lo_j, mid_j), jnp.where(ok, mid_j, hi_j)

            lo_j = jnp.full((SUBLANES, nq), -1, jnp.int32)
            hi_j = jnp.zeros((SUBLANES, nq), jnp.int32) + (n_kt * kt_rows - 1)
            n_steps = int(np.ceil(np.log2(score_ref.shape[0] * kt_rows))) + 1
            _, hi_j = lax.fori_loop(0, n_steps, tb_body, (lo_j, hi_j))
            return jnp.where(tie > 0, hi_j, big)

        jcut = lax.cond(jnp.sum(tie) > 0, tie_break, lambda: big)

        def mask_body(t, carry):
            blk = score_ref[t]
            sel = (blk >= thr) & ((blk > thr) | (kidx + t * kt_rows <= jcut))
            score_ref[t] = jnp.where(sel, 0.0, NEG_BIG)
            return carry
        lax.fori_loop(0, n_kt, mask_body, 0)

        m_ref[...] = jnp.full(m_ref.shape, NEG_BIG, F32)
        acc_ref[...] = jnp.zeros(acc_ref.shape, F32)

    bias = score_ref[kt]
    qa = qa_ref[0]
    q_heads = []
    for g in range(N_PAIRS):
        q_heads.extend(_head_halves(qa[:, g * LANES:(g + 1) * LANES]))
    for h0 in range(0, N_HEADS, HEADS_PER_STAGE):
        heads = range(h0, h0 + HEADS_PER_STAGE)
        logits = {h: _nt_dot(k_ref[0, :, (h // 2) * LANES:(h // 2 + 1) * LANES], q_heads[h])
                  .reshape(ns, SUBLANES, nq) + bias for h in heads}
        m_old = {h: m_ref[h] for h in heads}
        m_new = {h: jnp.maximum(m_old[h], jnp.max(_reduce0(logits[h], jnp.maximum),
                                                  axis=0, keepdims=True)) for h in heads}
        alpha = {h: jnp.exp2(m_old[h] - m_new[h]) for h in heads}
        p = {h: jnp.exp2(logits[h] - m_new[h]) for h in heads}
        pv = {h: jnp.dot(vt_ref[0, h * LANES:(h + 1) * LANES, :],
                         p[h].reshape(kt_rows, nq).astype(BF16), preferred_element_type=F32)
              for h in heads}
        for h in heads:
            acc_ref[h] = alpha[h][0:1] * acc_ref[h] + pv[h]
            m_ref[h] = m_new[h]

    @pl.when(last_of[i] == 1)
    def _finish():
        for g in range(N_PAIRS):
            outs = []
            for h in (2 * g, 2 * g + 1):
                acc = acc_ref[h]
                outs.append(acc[:HEAD_DIM] / acc[HEAD_DIM:HEAD_DIM + 1])
            o_t = jnp.concatenate(outs, axis=0)
            o_ref[0, :, g * LANES:(g + 1) * LANES] = o_t.T.astype(o_ref.dtype)


def _dsat(qa, qi, wi, ki16, k16, vt16, *, pos0, l_valid, qb_rows, kt_rows):
    bsz, tq, _ = qa.shape
    l_pad = k16.shape[1]
    assert tq % qb_rows == 0 and l_pad % kt_rows == 0 and qb_rows % LANES == 0
    topk = min(TOPK_MAX, l_valid // 4)
    tables = _dsa_tables(tq, qb_rows, kt_rows, pos0, l_valid)
    q_map = lambda b, i, qb_of, kt_of, f, l: (b, qb_of[i], 0)
    k_map = lambda b, i, qb_of, kt_of, f, l: (b, kt_of[i], 0)
    vt_map = lambda b, i, qb_of, kt_of, f, l: (b, 0, kt_of[i])
    all_map = lambda b, i, qb_of, kt_of, f, l: (b, 0, 0)
    grid_spec = pltpu.PrefetchScalarGridSpec(
        num_scalar_prefetch=4,
        grid=(bsz, tables[0].shape[0]),
        in_specs=[pl.BlockSpec((1, qb_rows, D_HEADS), q_map),
                  pl.BlockSpec((1, qb_rows, D_HEADS), q_map),
                  pl.BlockSpec((1, qb_rows, LANES), q_map),
                  pl.BlockSpec((1, l_pad, LANES), all_map),
                  pl.BlockSpec((1, kt_rows, D_HEADS), k_map),
                  pl.BlockSpec((1, N_HEADS * LANES, kt_rows), vt_map)],
        out_specs=pl.BlockSpec((1, qb_rows, D_HEADS), q_map),
        scratch_shapes=[pltpu.VMEM((l_pad // kt_rows, kt_rows // SUBLANES, SUBLANES, qb_rows), F32),
                        pltpu.VMEM((N_HEADS, SUBLANES, qb_rows), F32),
                        pltpu.VMEM((N_HEADS, LANES, qb_rows), F32)])
    return pl.pallas_call(
        functools.partial(_dsat_kernel, qb_rows=qb_rows, kt_rows=kt_rows, pos0=pos0,
                          l_valid=l_valid, topk=topk),
        grid_spec=grid_spec,
        out_shape=jax.ShapeDtypeStruct((bsz, tq, D_HEADS), BF16),
        compiler_params=pltpu.CompilerParams(dimension_semantics=("arbitrary", "arbitrary"),
                                             vmem_limit_bytes=VMEM_LIMIT),
        name="dsat",
    )(*tables, qa, qi, wi, ki16, k16, vt16)


def _dsa_tables(tq, qb_rows, kt_rows, pos0, l_valid):
    qb_l, kt_l, first_l, last_l = [], [], [], []
    for qb in range(tq // qb_rows):
        max_limit = min(((pos0 + qb * qb_rows + qb_rows - 1) // CHUNK + 1) * CHUNK, l_valid)
        n_kt = -(-max_limit // kt_rows)
        for kt in range(n_kt):
            qb_l.append(qb)
            kt_l.append(kt)
            first_l.append(int(kt == 0))
            last_l.append(int(kt == n_kt - 1))
    return [jnp.asarray(np.asarray(a, np.int32)) for a in (qb_l, kt_l, first_l, last_l)]


def _dsac(qa, qi, wi, ki_cache, ki_new, k_cache, v_cache, k_new, v_new, *, kt_rows):
    bsz, t_new, _ = qa.shape
    p_len = k_cache.shape[3]
    assert p_len % kt_rows == 0 and kt_rows % LANES == 0 and kt_rows > LANES
    assert t_new % 16 == 0 and t_new <= LANES
    n_ct = p_len // kt_rows
    topk = min(TOPK_MAX, (p_len + t_new) // 4)
    q_map = lambda b, j: (b, 0, 0)
    c_map = lambda b, j: (b, 0, 0, jnp.minimum(j, n_ct - 1))
    return pl.pallas_call(
        functools.partial(_dsac_kernel, n_ct=n_ct, kt_rows=kt_rows, t_new=t_new, p_len=p_len,
                          topk=topk),
        grid=(bsz, n_ct + 1),
        in_specs=[pl.BlockSpec((1, t_new, D_HEADS), q_map),
                  pl.BlockSpec((1, t_new, D_HEADS), q_map),
                  pl.BlockSpec((1, t_new, LANES), q_map),
                  pl.BlockSpec((1, IDX_DIM, p_len), q_map),
                  pl.BlockSpec((1, t_new, LANES), q_map),
                  pl.BlockSpec((1, N_HEADS, HEAD_DIM, kt_rows), c_map),
                  pl.BlockSpec((1, N_HEADS, HEAD_DIM, kt_rows), c_map),
                  pl.BlockSpec((1, t_new, D_HEADS), q_map),
                  pl.BlockSpec((1, t_new, D_HEADS), q_map)],
        out_specs=pl.BlockSpec((1, t_new, D_HEADS), q_map),
        out_shape=jax.ShapeDtypeStruct((bsz, t_new, D_HEADS), BF16),
        scratch_shapes=[pltpu.VMEM((n_ct + 1, t_new, kt_rows), F32),
                        pltpu.VMEM((t_new, LANES), F32),
                        pltpu.VMEM((t_new, LANES), jnp.int32),
                        pltpu.VMEM((N_HEADS, t_new, LANES), F32),
                        pltpu.VMEM((N_HEADS, t_new, LANES), F32),
                        pltpu.VMEM((N_HEADS, t_new, HEAD_DIM), F32)],
        compiler_params=pltpu.CompilerParams(dimension_semantics=("arbitrary", "arbitrary"),
                                             vmem_limit_bytes=VMEM_LIMIT),
        name="dsac",
    )(qa, qi, wi, ki_cache, ki_new, k_cache, v_cache, k_new, v_new)


def _sb_kernel(q_ref, k_ref, v_ref, o_ref, run_ref, acc_ref, *, qb_rows, pos0):
    qb = pl.program_id(1)
    qpos = pos0 + qb * qb_rows + lax.broadcasted_iota(jnp.int32, (qb_rows, 1), 0)
    kt_top = (pos0 + qb * qb_rows + qb_rows - 1) // LANES
    lane1 = lax.broadcasted_iota(jnp.int32, (1, LANES), 1)
    jj = lax.broadcasted_iota(jnp.int32, (LANES, 2 * LANES), 0)
    ss = lax.broadcasted_iota(jnp.int32, (LANES, 2 * LANES), 1)
    sum_rhs = jnp.where((jj > ss) | (ss >= LANES), 1.0, 0.0).astype(BF16)
    lane = lax.broadcasted_iota(jnp.int32, (qb_rows, LANES), 1)

    q = q_ref[0]
    q_heads = []
    for g in range(N_PAIRS):
        q_heads.extend(_head_halves(q[:, g * LANES:(g + 1) * LANES]))
    run_ref[...] = jnp.zeros(run_ref.shape, F32)
    acc_ref[...] = jnp.zeros(acc_ref.shape, F32)

    def cond(c):
        return c[0] > 0

    def body(c):
        _, kt = c
        rows = pl.ds(pl.multiple_of(kt * LANES, LANES), LANES)
        strict = (kt * LANES + lane1) < qpos
        heads = range(N_HEADS)
        k_t = [k_ref[0, rows, g * LANES:(g + 1) * LANES] for g in range(N_PAIRS)]
        v_t = [v_ref[0, rows, g * LANES:(g + 1) * LANES] for g in range(N_PAIRS)]
        z = [_nt_dot(q_heads[h], k_t[h // 2]) for h in heads]
        sp = [jnp.maximum(z[h], 0.0) + jnp.log1p(jnp.exp(-jnp.abs(z[h]))) for h in heads]
        ls = [jnp.where(strict, -sp[h], 0.0) for h in heads]
        ls_hi = [ls[h].astype(BF16) for h in heads]
        ls_lo = [(ls[h] - ls_hi[h].astype(F32)).astype(BF16) for h in heads]
        sums = [jnp.dot(ls_hi[h], sum_rhs, preferred_element_type=F32)
                + jnp.dot(ls_lo[h], sum_rhs, preferred_element_type=F32) for h in heads]
        run_old = [run_ref[h] for h in heads]
        wgt = [jnp.where(strict, jnp.exp(z[h] - sp[h] + run_old[h] + sums[h][:, :LANES]), 0.0)
               for h in heads]
        pv = [jnp.dot(wgt[h].astype(BF16), v_t[h // 2], preferred_element_type=F32) for h in heads]
        run_new = [run_old[h] + sums[h][:, LANES:] for h in heads]
        alive = run_new[0]
        for h in range(1, N_HEADS):
            alive = jnp.maximum(alive, run_new[h])
        for h in heads:
            acc_ref[h] += pv[h]
            run_ref[h] = run_new[h]
        go = jnp.where((kt > 0) & (jnp.max(alive) > SB_DEAD), 1, 0)
        return go, kt - 1

    lax.while_loop(cond, body, (jnp.int32(1), kt_top))
    for g in range(N_PAIRS):
        o_ref[0, :, g * LANES:(g + 1) * LANES] = jnp.where(
            lane < HEAD_DIM, acc_ref[2 * g], acc_ref[2 * g + 1]).astype(o_ref.dtype)


def _sb(qb16, k16, v16, *, pos0, qb_rows):
    bsz, tq, _ = qb16.shape
    l_pad = k16.shape[1]
    assert tq % qb_rows == 0 and l_pad % LANES == 0
    q_map = lambda b, i: (b, i, 0)
    all_map = lambda b, i: (b, 0, 0)
    resident = dict(pipeline_mode=pl.Buffered(1)) if bsz == 1 else {}
    return pl.pallas_call(
        functools.partial(_sb_kernel, qb_rows=qb_rows, pos0=pos0),
        grid=(bsz, tq // qb_rows),
        in_specs=[pl.BlockSpec((1, qb_rows, D_HEADS), q_map),
                  pl.BlockSpec((1, l_pad, D_HEADS), all_map, **resident),
                  pl.BlockSpec((1, l_pad, D_HEADS), all_map, **resident)],
        out_specs=pl.BlockSpec((1, qb_rows, D_HEADS), q_map),
        out_shape=jax.ShapeDtypeStruct((bsz, tq, D_HEADS), BF16),
        scratch_shapes=[pltpu.VMEM((N_HEADS, qb_rows, LANES), F32),
                        pltpu.VMEM((N_HEADS, qb_rows, LANES), F32)],
        compiler_params=pltpu.CompilerParams(dimension_semantics=("arbitrary", "arbitrary"),
                                             vmem_limit_bytes=VMEM_LIMIT),
        name="sb",
    )(qb16, k16, v16)


def _sbc_kernel(q_ref, kn_ref, vn_ref, kc_hbm, vc_hbm, o_ref, kbuf, vbuf, sem, run_ref, acc_ref,
                *, t_new, p_len):
    b = pl.program_id(0)
    n_ct = p_len // LANES
    qpos = p_len + lax.broadcasted_iota(jnp.int32, (t_new, 1), 0)
    lane1 = lax.broadcasted_iota(jnp.int32, (1, LANES), 1)
    jj = lax.broadcasted_iota(jnp.int32, (LANES, 2 * LANES), 0)
    ss = lax.broadcasted_iota(jnp.int32, (LANES, 2 * LANES), 1)
    sum_rhs = jnp.where((jj > ss) | (ss >= LANES), 1.0, 0.0).astype(BF16)

    def tile_copies(kt, slot):
        keys = pl.ds(pl.multiple_of(kt * LANES, LANES), LANES)
        return (pltpu.make_async_copy(kc_hbm.at[b, :, :, keys], kbuf.at[slot], sem.at[0, slot]),
                pltpu.make_async_copy(vc_hbm.at[b, :, :, keys], vbuf.at[slot], sem.at[1, slot]))

    for cp in tile_copies(n_ct - 1, 0):
        cp.start()

    q = q_ref[0]
    q_heads = [q[:, h * HEAD_DIM:(h + 1) * HEAD_DIM] for h in range(N_HEADS)]
    run_ref[...] = jnp.zeros(run_ref.shape, F32)
    acc_ref[...] = jnp.zeros(acc_ref.shape, F32)

    def tile_update(kt, z_of_head, pv_of_head):
        strict = (kt * LANES + lane1) < qpos
        heads = range(N_HEADS)
        z = [z_of_head(h) for h in heads]
        sp = [jnp.maximum(z[h], 0.0) + jnp.log1p(jnp.exp(-jnp.abs(z[h]))) for h in heads]
        ls = [jnp.where(strict, -sp[h], 0.0) for h in heads]
        ls_hi = [ls[h].astype(BF16) for h in heads]
        ls_lo = [(ls[h] - ls_hi[h].astype(F32)).astype(BF16) for h in heads]
        sums = [jnp.dot(ls_hi[h], sum_rhs, preferred_element_type=F32)
                + jnp.dot(ls_lo[h], sum_rhs, preferred_element_type=F32) for h in heads]
        run_old = [run_ref[h] for h in heads]
        wgt = [jnp.where(strict, jnp.exp(z[h] - sp[h] + run_old[h] + sums[h][:, :LANES]), 0.0)
               for h in heads]
        pv = [pv_of_head(h, wgt[h].astype(BF16)) for h in heads]
        run_new = [run_old[h] + sums[h][:, LANES:] for h in heads]
        alive = run_new[0]
        for h in range(1, N_HEADS):
            alive = jnp.maximum(alive, run_new[h])
        for h in heads:
            acc_ref[h] += pv[h]
            run_ref[h] = run_new[h]
        return jnp.max(alive)

    def new_head(x, h):
        return jnp.concatenate([x[:, h * HEAD_DIM:(h + 1) * HEAD_DIM],
                                jnp.zeros((LANES - t_new, HEAD_DIM), x.dtype)], axis=0)

    kn = kn_ref[0]
    vn = vn_ref[0]
    top = tile_update(n_ct,
                      lambda h: _nt_dot(q_heads[h], new_head(kn, h)),
                      lambda h, w16: jnp.dot(w16, new_head(vn, h), preferred_element_type=F32))

    def cond(c):
        return c[0] > 0

    def body(c):
        _, kt, slot, _ = c
        for cp in tile_copies(kt, slot):
            cp.wait()

        @pl.when(kt > 0)
        def _prefetch():
            for cp in tile_copies(kt - 1, 1 - slot):
                cp.start()

        top = tile_update(
            kt,
            lambda h: jnp.dot(q_heads[h], kbuf[slot, h].astype(BF16), preferred_element_type=F32),
            lambda h, w16: _nt_dot(w16, vbuf[slot, h].astype(BF16)))
        more = kt > 0
        go = jnp.where(more & (top > SB_DEAD), 1, 0)
        return go, kt - 1, 1 - slot, jnp.where(more, 1, 0)

    state = (jnp.where(top > SB_DEAD, 1, 0), jnp.int32(n_ct - 1), jnp.int32(0), jnp.int32(1))
    _, kt_left, slot_left, in_flight = lax.while_loop(cond, body, state)

    @pl.when(in_flight == 1)
    def _drain():
        for cp in tile_copies(kt_left, slot_left):
            cp.wait()

    for h in range(N_HEADS):
        o_ref[0, :, h * HEAD_DIM:(h + 1) * HEAD_DIM] = acc_ref[h].astype(o_ref.dtype)


def _sbc(qb16, k_new, v_new, k_cache, v_cache):
    bsz, t_new, _ = qb16.shape
    p_len = k_cache.shape[3]
    assert p_len % LANES == 0 and p_len >= LANES and t_new % 16 == 0 and t_new <= LANES
    q_map = lambda b: (b, 0, 0)
    return pl.pallas_call(
        functools.partial(_sbc_kernel, t_new=t_new, p_len=p_len),
        grid=(bsz,),
        in_specs=[pl.BlockSpec((1, t_new, D_HEADS), q_map),
                  pl.BlockSpec((1, t_new, D_HEADS), q_map),
                  pl.BlockSpec((1, t_new, D_HEADS), q_map),
                  pl.BlockSpec(memory_space=pl.ANY),
                  pl.BlockSpec(memory_space=pl.ANY)],
        out_specs=pl.BlockSpec((1, t_new, D_HEADS), q_map),
        out_shape=jax.ShapeDtypeStruct((bsz, t_new, D_HEADS), BF16),
        scratch_shapes=[pltpu.VMEM((2, N_HEADS, HEAD_DIM, LANES), F32),
                        pltpu.VMEM((2, N_HEADS, HEAD_DIM, LANES), F32),
                        pltpu.SemaphoreType.DMA((2, 2)),
                        pltpu.VMEM((N_HEADS, t_new, LANES), F32),
                        pltpu.VMEM((N_HEADS, t_new, HEAD_DIM), F32)],
        compiler_params=pltpu.CompilerParams(dimension_semantics=("arbitrary",),
                                             vmem_limit_bytes=VMEM_LIMIT),
        name="sbc",
    )(qb16, k_new, v_new, k_cache, v_cache)


def _post_kernel(oa_ref, ob_ref, sga_ref, sgb_ref, x_ref, wa_ref, wb_ref, wo_ref, g_ref,
                 wr_hi_ref, wr_lo_ref, br_ref, h_o, hn_o, comb_o, *, tm):
    ya = jnp.dot(oa_ref[...], wa_ref[...], preferred_element_type=F32)
    yb = jnp.dot(ob_ref[...], wb_ref[...], preferred_element_type=F32)
    mix = sga_ref[...] * ya + sgb_ref[...] * yb
    h = x_ref[...] + jnp.dot(mix.astype(BF16), wo_ref[...], preferred_element_type=F32)
    h_o[...] = h
    r = lax.rsqrt(jnp.mean(h * h, axis=-1, keepdims=True) + RMS_EPS)
    hn = (h * r) * g_ref[...]
    hn_hi = hn.astype(BF16)
    hn_o[...] = hn_hi
    hn_lo = (hn - hn_hi.astype(F32)).astype(BF16)
    logits = (jnp.dot(hn_hi, wr_hi_ref[...], preferred_element_type=F32)
              + jnp.dot(hn_lo, wr_hi_ref[...], preferred_element_type=F32)
              + jnp.dot(hn_hi, wr_lo_ref[...], preferred_element_type=F32)) + br_ref[...]

    lane = lax.broadcasted_iota(jnp.int32, (tm, LANES), 1)
    is_g = (lane >= N_EXPERTS) & (lane < N_EXPERTS + N_GROUPS)
    gl = jnp.where(is_g, logits, -jnp.inf)
    gmax = jnp.max(gl, axis=1, keepdims=True)
    g_lane = jnp.min(jnp.where(gl == gmax, lane, 2 ** 30), axis=1, keepdims=True)
    g_w = 1.0 / jnp.sum(jnp.exp(gl - gmax), axis=1, keepdims=True)
    in_grp = (lane < N_EXPERTS) & ((lane // EXPERTS_PER_GROUP) == (g_lane - N_EXPERTS))
    e1 = jnp.where(in_grp, logits, -jnp.inf)
    v1 = jnp.max(e1, axis=1, keepdims=True)
    i1 = jnp.min(jnp.where(e1 == v1, lane, 2 ** 30), axis=1, keepdims=True)
    e2 = jnp.where(lane == i1, -jnp.inf, e1)
    v2 = jnp.max(e2, axis=1, keepdims=True)
    i2 = jnp.min(jnp.where(e2 == v2, lane, 2 ** 30), axis=1, keepdims=True)
    t2 = jnp.exp(v2 - v1)
    den = 1.0 + t2
    comb_o[...] = jnp.where(lane == i1, (1.0 / den) * g_w,
                            jnp.where(lane == i2, (t2 / den) * g_w, 0.0))


def _post(oa, ob, sga, sgb, x2d, wa16, wb16, wo16, g_ffn, wr_hi, wr_lo, br):
    n = x2d.shape[0]
    tm = min(256, n)
    assert n % tm == 0
    row = lambda i: (i, 0)
    fix = lambda i: (0, 0)
    return pl.pallas_call(
        functools.partial(_post_kernel, tm=tm),
        grid=(n // tm,),
        in_specs=[pl.BlockSpec((tm, D_HEADS), row), pl.BlockSpec((tm, D_HEADS), row),
                  pl.BlockSpec((tm, D_MODEL), row), pl.BlockSpec((tm, D_MODEL), row),
                  pl.BlockSpec((tm, D_MODEL), row),
                  pl.BlockSpec((D_HEADS, D_MODEL), fix), pl.BlockSpec((D_HEADS, D_MODEL), fix),
                  pl.BlockSpec((D_MODEL, D_MODEL), fix), pl.BlockSpec((1, D_MODEL), fix),
                  pl.BlockSpec((D_MODEL, LANES), fix), pl.BlockSpec((D_MODEL, LANES), fix),
                  pl.BlockSpec((1, LANES), fix)],
        out_specs=[pl.BlockSpec((tm, D_MODEL), row), pl.BlockSpec((tm, D_MODEL), row),
                   pl.BlockSpec((tm, LANES), row)],
        out_shape=[jax.ShapeDtypeStruct((n, D_MODEL), F32),
                   jax.ShapeDtypeStruct((n, D_MODEL), BF16),
                   jax.ShapeDtypeStruct((n, LANES), F32)],
        compiler_params=pltpu.CompilerParams(dimension_semantics=("arbitrary",),
                                             vmem_limit_bytes=VMEM_LIMIT),
        name="post",
    )(oa, ob, sga, sgb, x2d, wa16, wb16, wo16, g_ffn, wr_hi, wr_lo, br)


MOE_ROWS = 256
ROW_ALIGN = 16


def _moe_kernel(hn_ref, comb_ref, h_ref, wg_ref, wu_ref, wd_ref, gf_ref, y_o,
                xs_ref, ws_ref, accs_ref, tri_ref, pos_ref, meta_ref, *, tm):
    e = pl.program_id(1)
    lane = lax.broadcasted_iota(jnp.int32, (tm, LANES), 1)

    @pl.when((pl.program_id(0) == 0) & (e == 0))
    def _constants():
        r = lax.broadcasted_iota(jnp.int32, (tm, tm), 0)
        c = lax.broadcasted_iota(jnp.int32, (tm, tm), 1)
        tri_ref[...] = jnp.where(r >= c, 1.0, 0.0).astype(BF16)

    @pl.when(e == 0)
    def _permute():
        comb = comb_ref[...]
        used = jnp.where(comb != 0.0, 1.0, 0.0)
        in_group = jnp.zeros((tm, LANES), F32)
        rest = jnp.zeros((tm, 1), F32)
        for g in range(N_GROUPS - 1, 0, -1):
            sel = (lane >= g * EXPERTS_PER_GROUP) & (lane < (g + 1) * EXPERTS_PER_GROUP)
            m_g = jnp.max(jnp.where(sel, used, 0.0), axis=1, keepdims=True) * (1.0 - rest)
            in_group = jnp.where(lane == g, m_g, in_group)
            rest = rest + m_g
        in_group = jnp.where(lane == 0, 1.0 - rest, in_group)
        rank = jnp.dot(tri_ref[...], in_group.astype(BF16), preferred_element_type=F32)
        total = rank[tm - 1:tm, :]
        counts = [jnp.sum(jnp.where(lane[:1] == g, total, 0.0)).astype(jnp.int32)
                  for g in range(N_GROUPS)]
        start = jnp.int32(0)
        start_vec = jnp.zeros((1, LANES), F32)
        for g in range(N_GROUPS):
            meta_ref[g] = start
            meta_ref[N_GROUPS + g] = counts[g]
            start_vec = jnp.where(lane[:1] == g, start.astype(F32), start_vec)
            start = start + counts[g]
        pos = jnp.sum(in_group * (start_vec + rank - 1.0), axis=1, keepdims=True)
        pos_b = jnp.broadcast_to(pos, (tm, LANES))
        pos_ref[...] = pos_b
        pos_row = pos_b.T[0:1, :]
        slot = lax.broadcasted_iota(jnp.int32, (tm, tm), 0).astype(F32)
        perm = jnp.where(pos_row == slot, 1.0, 0.0).astype(BF16)
        xs_ref[0:tm, :] = jnp.dot(perm, hn_ref[...], preferred_element_type=F32).astype(BF16)
        c_hi = comb.astype(BF16)
        r1 = comb - c_hi.astype(F32)
        c_mid = r1.astype(BF16)
        c_lo = (r1 - c_mid.astype(F32)).astype(BF16)
        ws_ref[0:tm, :] = (jnp.dot(perm, c_hi, preferred_element_type=F32)
                           + jnp.dot(perm, c_mid, preferred_element_type=F32)
                           + jnp.dot(perm, c_lo, preferred_element_type=F32))
        xs_ref[tm:, :] = jnp.zeros((MOE_ROWS, D_MODEL), BF16)
        ws_ref[tm:, :] = jnp.zeros((MOE_ROWS, LANES), F32)
        accs_ref[...] = jnp.zeros(accs_ref.shape, F32)

    g = e // EXPERTS_PER_GROUP
    start = meta_ref[g]
    count = meta_ref[N_GROUPS + g]
    first = (start // ROW_ALIGN) * ROW_ALIGN
    n_blocks = (start + count - first + MOE_ROWS - 1) // MOE_ROWS
    lane_b = lax.broadcasted_iota(jnp.int32, (MOE_ROWS, LANES), 1)

    def block(j, carry):
        rows = pl.ds(pl.multiple_of(first + j * MOE_ROWS, ROW_ALIGN), MOE_ROWS)
        x = xs_ref[rows, :]
        w = jnp.sum(jnp.where(lane_b == e, ws_ref[rows, :], 0.0), axis=1, keepdims=True)
        gate = jnp.dot(x, wg_ref[0], preferred_element_type=F32)
        up = jnp.dot(x, wu_ref[0], preferred_element_type=F32)
        hid = (gate * jax.nn.sigmoid(gate)) * up * w
        accs_ref[rows, :] += jnp.dot(hid.astype(BF16), wd_ref[0], preferred_element_type=F32)
        return carry
    lax.fori_loop(0, jnp.where(count > 0, n_blocks, 0), block, 0)

    @pl.when(e == N_EXPERTS - 1)
    def _finish():
        slot = lax.broadcasted_iota(jnp.int32, (tm, tm), 1).astype(F32)
        back = jnp.where(pos_ref[:, 0:1] == slot, 1.0, 0.0).astype(BF16)
        ffn = jnp.dot(back, accs_ref[0:tm, :].astype(BF16), preferred_element_type=F32)
        out = h_ref[...] + ffn
        r = lax.rsqrt(jnp.mean(out * out, axis=-1, keepdims=True) + RMS_EPS)
        y_o[...] = (out * r) * gf_ref[...]


def _moe(hn16, comb, h, wg, wu, wd, g_final):
    n = h.shape[0]
    tm = min(1024, n)
    assert n % tm == 0 and tm % LANES == 0
    row = lambda i, e: (i, 0)
    exp = lambda i, e: (e, 0, 0)
    fix = lambda i, e: (0, 0)
    return pl.pallas_call(
        functools.partial(_moe_kernel, tm=tm),
        grid=(n // tm, N_EXPERTS),
        in_specs=[pl.BlockSpec((tm, D_MODEL), row), pl.BlockSpec((tm, LANES), row),
                  pl.BlockSpec((tm, D_MODEL), row),
                  pl.BlockSpec((1, D_MODEL, D_FF_EXPERT), exp),
                  pl.BlockSpec((1, D_MODEL, D_FF_EXPERT), exp),
                  pl.BlockSpec((1, D_FF_EXPERT, D_MODEL), exp),
                  pl.BlockSpec((1, D_MODEL), fix)],
        out_specs=pl.BlockSpec((tm, D_MODEL), row),
        out_shape=jax.ShapeDtypeStruct((n, D_MODEL), F32),
        scratch_shapes=[pltpu.VMEM((tm + MOE_ROWS, D_MODEL), BF16),
                        pltpu.VMEM((tm + MOE_ROWS, LANES), F32),
                        pltpu.VMEM((tm + MOE_ROWS, D_MODEL), F32),
                        pltpu.VMEM((tm, tm), BF16),
                        pltpu.VMEM((tm, LANES), F32),
                        pltpu.SMEM((2 * N_GROUPS,), jnp.int32)],
        compiler_params=pltpu.CompilerParams(dimension_semantics=("arbitrary", "arbitrary"),
                                             vmem_limit_bytes=VMEM_LIMIT),
        name="moe",
    )(hn16, comb, h, wg, wu, wd, g_final)


def _prep_weights(norm_mix_g, w_in, w_br_a, w_br_b, w_out, norm_ffn_g, w_rg, b_rg, w_re, b_re,
                  w_eg, w_eu, w_ed, norm_final_g):
    offs = np.cumsum(IN_SPLIT)[:-1].tolist()
    q_a, k_a, v_a, q_i, k_i, w_i, q_b, k_b, v_b, g_a, g_b = jnp.split(w_in, offs, axis=-1)
    w_i_pad = jnp.pad(w_i, ((0, 0), (0, LANES - N_IDX_HEADS)))
    w16 = jnp.concatenate([q_a, k_a, v_a, q_i, k_i, k_i, w_i_pad, q_b, k_b, v_b, g_a, g_b],
                          axis=1).astype(BF16)
    half = ROT_HALF
    inv_freq = ROPE_THETA ** (-jnp.arange(half, dtype=F32) / half)
    d = np.arange(LANES) % HEAD_DIM
    invf = jnp.where(jnp.asarray(d < ROT_DIM), inv_freq[jnp.asarray(d % half)], 0.0)[None, :]
    pad = LANES - N_EXPERTS - N_GROUPS
    w_r = jnp.pad(jnp.concatenate([w_re, w_rg], axis=1), ((0, 0), (0, pad)))
    wr_hi = w_r.astype(BF16)
    wr_lo = (w_r - wr_hi.astype(F32)).astype(BF16)
    b_r = jnp.pad(jnp.concatenate([b_re, b_rg]), (0, pad))[None, :]
    return dict(g_mix=norm_mix_g[None, :], w16=w16, invf=invf,
                wa16=w_br_a.astype(BF16), wb16=w_br_b.astype(BF16), wo16=w_out.astype(BF16),
                g_ffn=norm_ffn_g[None, :], wr_hi=wr_hi, wr_lo=wr_lo, b_r=b_r,
                wg=w_eg.astype(BF16), wu=w_eu.astype(BF16), wd=w_ed.astype(BF16),
                g_final=norm_final_g[None, :])


def _layer(x, past, wts, *, dsa_qb, dsa_kt, sb_qb):
    bsz, t, _ = x.shape
    n = bsz * t
    x2d = x.reshape(n, D_MODEL)
    pos0 = 0 if past is None else past[0].shape[1]
    qa_scale = ATT_SCALE * LOG2_E if past is None else ATT_SCALE
    pr = _proj(x2d, wts["g_mix"], wts["w16"], wts["invf"], tq=t, pos0=pos0, qa_scale=qa_scale)
    shp = lambda a: a.reshape(bsz, t, a.shape[-1])

    if past is None:
        vt = pr["va16"].reshape(bsz, t, N_HEADS, HEAD_DIM).transpose(0, 2, 3, 1)
        vt = jnp.concatenate([vt, jnp.ones((bsz, N_HEADS, 1, t), BF16),
                              jnp.zeros((bsz, N_HEADS, LANES - HEAD_DIM - 1, t), BF16)], axis=2)
        oa = _dsat(shp(pr["qa"]), shp(pr["qi"]), shp(pr["wi"]), shp(pr["ki16"]), shp(pr["ka16"]),
                   vt.reshape(bsz, N_HEADS * LANES, t), pos0=0, l_valid=t,
                   qb_rows=dsa_qb, kt_rows=dsa_kt)
        ob = _sb(shp(pr["qb"]), shp(pr["kb16"]), shp(pr["vb16"]), pos0=0, qb_rows=sb_qb)
    else:
        keys_last = lambda c: jnp.transpose(c, (0, 2, 3, 1))
        oa = _dsac(shp(pr["qa"]), shp(pr["qi"]), shp(pr["wi"]), jnp.swapaxes(past[2], 1, 2),
                   shp(pr["ki16"]), keys_last(past[0]), keys_last(past[1]),
                   shp(pr["ka16"]), shp(pr["va16"]), kt_rows=dsa_kt)
        ob = _sbc(shp(pr["qb"]), shp(pr["kb16"]), shp(pr["vb16"]),
                  keys_last(past[3]), keys_last(past[4]))
    h, hn16, comb = _post(oa.reshape(n, D_HEADS), ob.reshape(n, D_HEADS), pr["sga"], pr["sgb"],
                          x2d, wts["wa16"], wts["wb16"], wts["wo16"], wts["g_ffn"],
                          wts["wr_hi"], wts["wr_lo"], wts["b_r"])
    y = _moe(hn16, comb, h, wts["wg"], wts["wu"], wts["wd"], wts["g_final"])
    heads = lambda a: a.reshape(1, bsz, t, N_HEADS, HEAD_DIM)
    rows = (heads(pr["ka"]), heads(pr["va"]),
            pr["ki"][:, :IDX_DIM].reshape(1, bsz, t, IDX_DIM),
            heads(pr["kb"]), heads(pr["vb"]))
    return y.reshape(bsz, t, D_MODEL), rows


def kernel(x_prompt, x_sample, cache_a_k, cache_a_v, cache_idx_k, cache_b_k, cache_b_v,
           norm_mix_g, w_in, w_br_a, w_br_b, w_out, norm_ffn_g,
           w_router_group, b_router_group, w_router_expert, b_router_expert,
           w_exp_gate, w_exp_up, w_exp_down, norm_final_g):
    assert w_in.shape[0] == 1, "single-layer model"
    wts = _prep_weights(norm_mix_g[0], w_in[0], w_br_a[0], w_br_b[0], w_out[0], norm_ffn_g[0],
                        w_router_group[0], b_router_group[0], w_router_expert[0],
                        b_router_expert[0], w_exp_gate[0], w_exp_up[0], w_exp_down[0],
                        norm_final_g)
    y_p, rows_p = _layer(x_prompt, None, wts, dsa_qb=256, dsa_kt=1024, sb_qb=128)
    past = (cache_a_k[0], cache_a_v[0], cache_idx_k[0], cache_b_k[0], cache_b_v[0])
    y_s, rows_s = _layer(x_sample, past, wts, dsa_qb=None, dsa_kt=1024, sb_qb=None)
    return (y_p, y_s) + rows_p + rows_s
```
